```python
import math
import jax, jax.numpy as jnp
from jax import lax
import numpy as np

D_MODEL = 1024
BATCH = 16
SEQ = 2048
DEPTH = 1
DEC_BATCH = 8
DEC_SEQ = 2048
PAST_LEN = 128

N_HEADS = 8
HEAD_DIM = 64
V_HEAD_DIM = 2 * HEAD_DIM
QK_WIDTH = 2 * N_HEADS * HEAD_DIM
V_WIDTH = N_HEADS * V_HEAD_DIM
N_FOURIER_GROUPS = 4
FOURIER_GROUP = 128
FOURIER_WIDTH = N_FOURIER_GROUPS * FOURIER_GROUP
IN_WIDTH = 2 * QK_WIDTH + V_WIDTH + FOURIER_WIDTH
N_BRANCHES = 2
D_FF = 4 * D_MODEL
ROPE_THETA = 10000.0
Q_BLOCK = 128
EPS = 1e-6
LAMBDA_STD = 0.1

kernel_name = "gated_diffattn_fnet_encoder"


def rmsnorm(x, g):
    xf = x.astype(jnp.float32)
    y = xf * lax.rsqrt(jnp.mean(xf * xf, axis=-1, keepdims=True) + EPS)
    return (y * g.astype(jnp.float32)).astype(x.dtype)


def rope(x):
    S, Dh = x.shape[2], x.shape[3]
    half = Dh // 2
    freqs = ROPE_THETA ** (-jnp.arange(half, dtype=jnp.float32) * 2.0 / Dh)
    ang = jnp.arange(S, dtype=jnp.float32)[:, None] * freqs[None, :]
    cos, sin = jnp.cos(ang), jnp.sin(ang)
    xf = x.astype(jnp.float32)
    x1, x2 = xf[..., :half], xf[..., half:]
    out = jnp.concatenate([x1 * cos - x2 * sin, x2 * cos + x1 * sin], axis=-1)
    return out.astype(x.dtype)


def diff_attention(q, k, v, lam):
    B, H2, S, Dh = q.shape
    nb = S // Q_BLOCK
    qb = q.reshape(B, H2, nb, Q_BLOCK, Dh).transpose(2, 0, 1, 3, 4)
    scale = Dh ** -0.5

    def block(qblk):
        s = jnp.einsum('bhqd,bhkd->bhqk', qblk, k, preferred_element_type=jnp.float32) * scale
        p = jax.nn.softmax(s, axis=-1).reshape(B, N_HEADS, 2, Q_BLOCK, S)
        a = p[:, :, 0] - lam * p[:, :, 1]
        return jnp.einsum('bhqk,bhkd->bhqd', a.astype(v.dtype), v)

    o = lax.map(block, qb)
    return o.transpose(1, 2, 0, 3, 4).reshape(B, N_HEADS, S, V_HEAD_DIM)


def encoder_layer(x, lambda_init, g_mix, w_in, g_q, g_k, lam_q1, lam_k1, lam_q2, lam_k2,
                  g_sub, w_attn_br, w_four_br, w_gate, b_gate, w_out, g_mlp, w_up, w_down):
    B, S, _ = x.shape
    h = rmsnorm(x, g_mix)
    proj = h @ w_in
    q, k, v, f = jnp.split(proj, [QK_WIDTH, 2 * QK_WIDTH, 2 * QK_WIDTH + V_WIDTH], axis=-1)

    q = rmsnorm(q.reshape(B, S, 2 * N_HEADS, HEAD_DIM), g_q).transpose(0, 2, 1, 3)
    k = rmsnorm(k.reshape(B, S, 2 * N_HEADS, HEAD_DIM), g_k).transpose(0, 2, 1, 3)
    q, k = rope(q), rope(k)
    v = v.reshape(B, S, N_HEADS, V_HEAD_DIM).transpose(0, 2, 1, 3)
    lam = (jnp.exp(jnp.sum(lam_q1.astype(jnp.float32) * lam_k1.astype(jnp.float32)))
           - jnp.exp(jnp.sum(lam_q2.astype(jnp.float32) * lam_k2.astype(jnp.float32)))
           + lambda_init)
    o = diff_attention(q, k, v, lam)
    o = rmsnorm(o, g_sub) * (1.0 - lambda_init)
    o = o.transpose(0, 2, 1, 3).reshape(B, S, V_WIDTH)
    attn_out = o @ w_attn_br

    fg = f.reshape(B, S, N_FOURIER_GROUPS, FOURIER_GROUP).astype(jnp.float32)
    fr = jnp.real(jnp.fft.fft2(fg, axes=(1, 3), norm='ortho')).astype(x.dtype)
    four_out = fr.reshape(B, S, FOURIER_WIDTH) @ w_four_br

    gates = jax.nn.sigmoid((h @ w_gate + b_gate).astype(jnp.float32)).astype(x.dtype)
    gates = gates.reshape(B, S, N_BRANCHES, D_MODEL)
    mixed = gates[:, :, 0] * attn_out + gates[:, :, 1] * four_out
    x = x + mixed @ w_out

    h2 = rmsnorm(x, g_mlp)
    u = jnp.square(jax.nn.relu(h2 @ w_up))
    return x + u @ w_down


def setup_inputs(seed: int = 0) -> dict:
    key = jax.random.key(seed)
    ks = jax.random.split(key, 20)
    f32 = jnp.float32

    def nrm(k, shape, scale):
        return jax.random.normal(k, shape, f32) * scale

    def gain(k, shape):
        return 1.0 + 0.02 * jax.random.normal(k, shape, f32)

    L = DEPTH
    return {
        "x_prompt": jax.random.normal(ks[0], (BATCH, SEQ, D_MODEL), f32),
        "x_sample": jax.random.normal(ks[1], (DEC_BATCH, DEC_SEQ, D_MODEL), f32),
        "g_mix": gain(ks[2], (L, D_MODEL)),
        "w_in": nrm(ks[3], (L, D_MODEL, IN_WIDTH), D_MODEL ** -0.5),
        "g_q": gain(ks[4], (L, HEAD_DIM)),
        "g_k": gain(ks[5], (L, HEAD_DIM)),
        "lam_q1": nrm(ks[6], (L, HEAD_DIM), LAMBDA_STD),
        "lam_k1": nrm(ks[7], (L, HEAD_DIM), LAMBDA_STD),
        "lam_q2": nrm(ks[8], (L, HEAD_DIM), LAMBDA_STD),
        "lam_k2": nrm(ks[9], (L, HEAD_DIM), LAMBDA_STD),
        "g_sub": gain(ks[10], (L, V_HEAD_DIM)),
        "w_attn_br": nrm(ks[11], (L, V_WIDTH, D_MODEL), V_WIDTH ** -0.5),
        "w_four_br": nrm(ks[12], (L, FOURIER_WIDTH, D_MODEL), FOURIER_WIDTH ** -0.5),
        "w_gate": nrm(ks[13], (L, D_MODEL, N_BRANCHES * D_MODEL), D_MODEL ** -0.5),
        "b_gate": nrm(ks[14], (L, N_BRANCHES * D_MODEL), 0.02),
        "w_out": nrm(ks[15], (L, D_MODEL, D_MODEL), D_MODEL ** -0.5),
        "g_mlp": gain(ks[16], (L, D_MODEL)),
        "w_up": nrm(ks[17], (L, D_MODEL, D_FF), D_MODEL ** -0.5),
        "w_down": nrm(ks[18], (L, D_FF, D_MODEL), D_FF ** -0.5),
    }


def reference(x_prompt, x_sample, g_mix, w_in, g_q, g_k, lam_q1, lam_k1, lam_q2, lam_k2,
              g_sub, w_attn_br, w_four_br, w_gate, b_gate, w_out, g_mlp, w_up, w_down):
    yp, ys = x_prompt, x_sample
    for l in range(DEPTH):
        lambda_init = 0.8 - 0.6 * math.exp(-0.3 * l)
        params = (g_mix[l], w_in[l], g_q[l], g_k[l], lam_q1[l], lam_k1[l], lam_q2[l], lam_k2[l],
                  g_sub[l], w_attn_br[l], w_four_br[l], w_gate[l], b_gate[l], w_out[l],
                  g_mlp[l], w_up[l], w_down[l])
        yp = encoder_layer(yp, lambda_init, *params)
        ys = encoder_layer(ys, lambda_init, *params)
    return (yp, ys)
```

```python
import functools
import math

import jax
import jax.numpy as jnp
from jax import lax
from jax.experimental import pallas as pl
from jax.experimental.pallas import tpu as pltpu

D_MODEL = 1024
SEQ = 2048
N_HEADS = 8
HEAD_DIM = 64
V_HEAD_DIM = 2 * HEAD_DIM
QK_WIDTH = 2 * N_HEADS * HEAD_DIM
V_WIDTH = N_HEADS * V_HEAD_DIM
N_FOURIER_GROUPS = 4
FOURIER_GROUP = 128
FOURIER_WIDTH = N_FOURIER_GROUPS * FOURIER_GROUP
IN_WIDTH = 2 * QK_WIDTH + V_WIDTH + FOURIER_WIDTH
D_FF = 4 * D_MODEL
ROPE_THETA = 10000.0
EPS = 1e-6
LAMBDA_INIT = 0.8 - 0.6 * math.exp(-0.3 * 0)

LANES = 128
IN_TILE = 512
OUT_TILE = 256
Q_TILE = 256
DFT_ROWS = 512
MIB = 1024 * 1024

BF16 = jnp.bfloat16
F32 = jnp.float32


def _dot(a, b):
    return jnp.dot(a, b, preferred_element_type=F32)


def _rmsnorm_rows(x, g):
    ms = jnp.mean(x * x, axis=-1, keepdims=True)
    return x * lax.rsqrt(ms + EPS) * g


def _resident(shape):
    return pl.BlockSpec(shape, lambda *_: (0,) * len(shape), pipeline_mode=pl.Buffered(1))


def _qk_head_norm_rope(z, gain, cos, sin_signed, first_map, low_half):
    zz = z * z
    s1 = jnp.sum(jnp.where(first_map, zz, 0.0), axis=-1, keepdims=True)
    s2 = jnp.sum(jnp.where(first_map, 0.0, zz), axis=-1, keepdims=True)
    inv = jnp.where(first_map,
                    lax.rsqrt(s1 * (1.0 / HEAD_DIM) + EPS),
                    lax.rsqrt(s2 * (1.0 / HEAD_DIM) + EPS))
    zn = z * inv * gain
    half = HEAD_DIM // 2
    rot = jnp.where(low_half, pltpu.roll(zn, LANES - half, 1), pltpu.roll(zn, half, 1))
    return zn * cos + rot * sin_signed


def _inproj_kernel(x_ref, gmix_ref, w_ref, gq_ref, gk_ref, cos_ref, sin_ref, dc_ref,
                   q_ref, k_ref, v_ref, y1_ref, y2_ref):
    h = _rmsnorm_rows(x_ref[...], gmix_ref[...]).astype(BF16)
    lane = lax.broadcasted_iota(jnp.int32, (1, LANES), 1)
    first_map = lane < HEAD_DIM
    low_half = (lane % HEAD_DIM) < (HEAD_DIM // 2)
    cos = cos_ref[...]
    sin_signed = sin_ref[...]

    for off, g_ref, o_ref in ((0, gq_ref, q_ref), (QK_WIDTH, gk_ref, k_ref)):
        z = _dot(h, w_ref[:, off:off + QK_WIDTH])
        gain = g_ref[...]
        for hh in range(N_HEADS):
            sl = slice(hh * LANES, (hh + 1) * LANES)
            o_ref[:, sl] = _qk_head_norm_rope(z[:, sl], gain, cos, sin_signed,
                                              first_map, low_half).astype(BF16)

    v_ref[...] = _dot(h, w_ref[:, 2 * QK_WIDTH:2 * QK_WIDTH + V_WIDTH]).astype(BF16)

    f = _dot(h, w_ref[:, 2 * QK_WIDTH + V_WIDTH:IN_WIDTH]).astype(BF16)
    dc = dc_ref[...]
    for g in range(N_FOURIER_GROUPS):
        sl = slice(g * FOURIER_GROUP, (g + 1) * FOURIER_GROUP)
        y = _dot(f[:, sl], dc)
        y1_ref[:, sl] = y[:, :FOURIER_GROUP].astype(BF16)
        y2_ref[:, sl] = y[:, FOURIER_GROUP:].astype(BF16)


def _inproj(x, gmix, w_in, gq, gk, cos, sin_signed, dc):
    n_tok = x.shape[0]
    tm = IN_TILE
    pos_blocks = SEQ // tm
    tok = lambda w: pl.BlockSpec((tm, w), lambda i: (i, 0))
    rope = pl.BlockSpec((tm, LANES), lambda i: (i % pos_blocks, 0))
    out_shapes = (
        jax.ShapeDtypeStruct((n_tok, QK_WIDTH), BF16),
        jax.ShapeDtypeStruct((n_tok, QK_WIDTH), BF16),
        jax.ShapeDtypeStruct((n_tok, V_WIDTH), BF16),
        jax.ShapeDtypeStruct((n_tok, FOURIER_WIDTH), BF16),
        jax.ShapeDtypeStruct((n_tok, FOURIER_WIDTH), BF16),
    )
    return pl.pallas_call(
        _inproj_kernel,
        grid=(n_tok // tm,),
        in_specs=[tok(D_MODEL), _resident((1, D_MODEL)), _resident((D_MODEL, IN_WIDTH)),
                  _resident((1, LANES)), _resident((1, LANES)), rope, rope,
                  _resident((FOURIER_GROUP, 2 * FOURIER_GROUP))],
        out_specs=(tok(QK_WIDTH), tok(QK_WIDTH), tok(V_WIDTH), tok(FOURIER_WIDTH),
                   tok(FOURIER_WIDTH)),
        out_shape=out_shapes,
        compiler_params=pltpu.CompilerParams(
            dimension_semantics=("arbitrary",), vmem_limit_bytes=48 * MIB),
        name="inproj",
    )(x, gmix, w_in, gq, gk, cos, sin_signed, dc)


def _attn_kernel(lq1_ref, lk1_ref, lq2_ref, lk2_ref, gsub_ref, q_ref, k_ref, v_ref,
                 o_ref, lam_ref):
    @pl.when((pl.program_id(0) == 0) & (pl.program_id(1) == 0))
    def _():
        a1 = jnp.sum(lq1_ref[...] * lk1_ref[...], axis=-1, keepdims=True)
        a2 = jnp.sum(lq2_ref[...] * lk2_ref[...], axis=-1, keepdims=True)
        lam_ref[...] = jnp.exp(a1) - jnp.exp(a2) + LAMBDA_INIT

    lam = lam_ref[...]
    k = k_ref[0]
    v = v_ref[0]
    gsub = gsub_ref[...] * (1.0 - LAMBDA_INIT)
    lane = lax.broadcasted_iota(jnp.int32, (1, LANES), 1)
    first_map = lane < HEAD_DIM
    tq = Q_TILE

    def body(i, carry):
        rows = pl.ds(pl.multiple_of(i * tq, tq), tq)
        q = q_ref[0, rows, :]
        zero = jnp.zeros_like(q)
        qq = jnp.concatenate([jnp.where(first_map, q, zero), jnp.where(first_map, zero, q)],
                             axis=0)
        s = lax.dot_general(qq, k, (((1,), (1,)), ((), ())), preferred_element_type=F32)
        m = jnp.max(s, axis=-1, keepdims=True)
        e = jnp.exp(s - m)
        r = 1.0 / jnp.sum(e, axis=-1, keepdims=True)
        a = e[:tq] * r[:tq] - e[tq:] * (lam * r[tq:])
        o = _dot(a.astype(BF16), v)
        o_ref[0, rows, :] = _rmsnorm_rows(o, gsub).astype(BF16)
        return carry

    lax.fori_loop(0, SEQ // tq, body, 0)


def _attention(lq1, lk1, lq2, lk2, gsub, q, k, v):
    n_seq = q.shape[0]
    head = pl.BlockSpec((1, SEQ, LANES), lambda b, h: (b, 0, h))
    return pl.pallas_call(
        _attn_kernel,
        grid=(n_seq, N_HEADS),
        in_specs=[_resident((1, HEAD_DIM))] * 4 + [_resident((1, V_HEAD_DIM)), head, head, head],
        out_specs=head,
        out_shape=jax.ShapeDtypeStruct((n_seq, SEQ, V_WIDTH), BF16),
        scratch_shapes=[pltpu.VMEM((1, 1), F32)],
        compiler_params=pltpu.CompilerParams(
            dimension_semantics=("arbitrary", "arbitrary"), vmem_limit_bytes=40 * MIB),
        name="diff_attn",
    )(lq1, lk1, lq2, lk2, gsub, q, k, v)


def _seq_dft_kernel(cs_ref, nsn_ref, y1_ref, y2_ref, o_ref):
    y1 = y1_ref[0]
    y2 = y2_ref[0]

    def body(i, carry):
        rows = pl.ds(pl.multiple_of(i * DFT_ROWS, DFT_ROWS), DFT_ROWS)
        acc = _dot(cs_ref[rows, :], y1) + _dot(nsn_ref[rows, :], y2)
        o_ref[0, rows, :] = acc.astype(BF16)
        return carry

    lax.fori_loop(0, SEQ // DFT_ROWS, body, 0)


def _seq_dft(cs, nsn, y1, y2):
    n_seq = y1.shape[0]
    seq = pl.BlockSpec((1, SEQ, FOURIER_WIDTH), lambda b: (b, 0, 0))
    return pl.pallas_call(
        _seq_dft_kernel,
        grid=(n_seq,),
        in_specs=[_resident((SEQ, SEQ)), _resident((SEQ, SEQ)), seq, seq],
        out_specs=seq,
        out_shape=jax.ShapeDtypeStruct((n_seq, SEQ, FOURIER_WIDTH), BF16),
        compiler_params=pltpu.CompilerParams(
            dimension_semantics=("arbitrary",), vmem_limit_bytes=48 * MIB),
        name="seq_dft",
    )(cs, nsn, y1, y2)


def _merge_mlp_kernel(x_ref, att_ref, four_ref, gmix_ref, wgate_ref, bgate_ref, wattn_ref,
                      wfour_ref, wout_ref, gmlp_ref, wup_ref, wdown_ref, y_ref):
    x = x_ref[...]
    h = _rmsnorm_rows(x, gmix_ref[...]).astype(BF16)
    gates = jax.nn.sigmoid(_dot(h, wgate_ref[...]) + bgate_ref[...])
    attn_out = _dot(att_ref[...], wattn_ref[...])
    four_out = _dot(four_ref[...], wfour_ref[...])
    mixed = gates[:, :D_MODEL] * attn_out + gates[:, D_MODEL:] * four_out
    x1 = x + _dot(mixed.astype(BF16), wout_ref[...])
    h2 = _rmsnorm_rows(x1, gmlp_ref[...]).astype(BF16)
    u = jnp.square(jnp.maximum(_dot(h2, wup_ref[...]), 0.0)).astype(BF16)
    y_ref[...] = x1 + _dot(u, wdown_ref[...])


def _merge_mlp(x, att, four, gmix, wgate, bgate, wattn, wfour, wout, gmlp, wup, wdown):
    n_tok = x.shape[0]
    tm = OUT_TILE
    tok = lambda w: pl.BlockSpec((tm, w), lambda i: (i, 0))
    return pl.pallas_call(
        _merge_mlp_kernel,
        grid=(n_tok // tm,),
        in_specs=[tok(D_MODEL), tok(V_WIDTH), tok(FOURIER_WIDTH), _resident((1, D_MODEL)),
                  _resident((D_MODEL, 2 * D_MODEL)), _resident((1, 2 * D_MODEL)),
                  _resident((V_WIDTH, D_MODEL)), _resident((FOURIER_WIDTH, D_MODEL)),
                  _resident((D_MODEL, D_MODEL)), _resident((1, D_MODEL)),
                  _resident((D_MODEL, D_FF)), _resident((D_FF, D_MODEL))],
        out_specs=tok(D_MODEL),
        out_shape=jax.ShapeDtypeStruct((n_tok, D_MODEL), F32),
        compiler_params=pltpu.CompilerParams(
            dimension_semantics=("arbitrary",), vmem_limit_bytes=56 * MIB),
        name="merge_mlp",
    )(x, att, four, gmix, wgate, bgate, wattn, wfour, wout, gmlp, wup, wdown)


def _rope_tables():
    half = HEAD_DIM // 2
    freqs = ROPE_THETA ** (-jnp.arange(half, dtype=F32) * 2.0 / HEAD_DIM)
    ang = jnp.arange(SEQ, dtype=F32)[:, None] * freqs[None, :]
    cos = jnp.tile(jnp.cos(ang), (1, LANES // half))
    sin = jnp.tile(jnp.sin(ang), (1, LANES // half))
    lane = jnp.arange(LANES)
    sign = jnp.where((lane % HEAD_DIM) < half, -1.0, 1.0).astype(F32)
    return cos, sin * sign[None, :]


def _dft_angles(n):
    idx = jnp.arange(n, dtype=jnp.int32)
    kn = (idx[:, None] * idx[None, :]) % n
    return kn.astype(F32) * (2.0 * math.pi / n)


def _dft_tables():
    norm = 1.0 / math.sqrt(SEQ * FOURIER_GROUP)
    ang_s = _dft_angles(SEQ)
    cs = (jnp.cos(ang_s) * norm).astype(BF16)
    nsn = (jnp.sin(ang_s) * (-norm)).astype(BF16)
    ang_c = _dft_angles(FOURIER_GROUP)
    dc = jnp.concatenate([jnp.cos(ang_c), jnp.sin(ang_c)], axis=1).astype(BF16)
    return cs, nsn, dc


def kernel(x_prompt, x_sample, g_mix, w_in, g_q, g_k, lam_q1, lam_k1, lam_q2, lam_k2, g_sub,
           w_attn_br, w_four_br, w_gate, b_gate, w_out, g_mlp, w_up, w_down):
    row = lambda p: p[0].reshape(1, -1).astype(F32)
    as_bf16 = lambda w: w[0].astype(BF16)
    cos, sin_signed = _rope_tables()
    cs, nsn, dc = _dft_tables()
    gq = jnp.tile(row(g_q), (1, LANES // HEAD_DIM)) * (HEAD_DIM ** -0.5)
    gk = jnp.tile(row(g_k), (1, LANES // HEAD_DIM))
    gmix = row(g_mix)
    lams = (row(lam_q1), row(lam_k1), row(lam_q2), row(lam_k2))
    w_in_b, w_gate_b, w_attn_b, w_four_b, w_out_b, w_up_b, w_down_b = map(
        as_bf16, (w_in, w_gate, w_attn_br, w_four_br, w_out, w_up, w_down))

    def layer(x3):
        n_seq = x3.shape[0]
        x = x3.reshape(n_seq * SEQ, D_MODEL)
        q, k, v, y1, y2 = _inproj(x, gmix, w_in_b, gq, gk, cos, sin_signed, dc)
        seq3 = lambda a: a.reshape(n_seq, SEQ, a.shape[-1])
        att = _attention(*lams, row(g_sub), seq3(q), seq3(k), seq3(v))
        four = _seq_dft(cs, nsn, seq3(y1), seq3(y2))
        y = _merge_mlp(x, att.reshape(n_seq * SEQ, V_WIDTH),
                       four.reshape(n_seq * SEQ, FOURIER_WIDTH), gmix, w_gate_b, row(b_gate),
                       w_attn_b, w_four_b, w_out_b, row(g_mlp), w_up_b, w_down_b)
        return y.reshape(n_seq, SEQ, D_MODEL)

    return layer(x_prompt), layer(x_sample)
```

```python
import functools
import math

import jax
import jax.numpy as jnp
from jax import lax
from jax.experimental import pallas as pl
from jax.experimental.pallas import tpu as pltpu

D_MODEL = 1024
SEQ = 2048
N_HEADS = 8
HEAD_DIM = 64
V_HEAD_DIM = 2 * HEAD_DIM
QK_WIDTH = 2 * N_HEADS * HEAD_DIM
V_WIDTH = N_HEADS * V_HEAD_DIM
N_FOURIER_GROUPS = 4
FOURIER_GROUP = 128
FOURIER_WIDTH = N_FOURIER_GROUPS * FOURIER_GROUP
IN_WIDTH = 2 * QK_WIDTH + V_WIDTH + FOURIER_WIDTH
D_FF = 4 * D_MODEL
ROPE_THETA = 10000.0
EPS = 1e-6
LAMBDA_INIT = 0.8 - 0.6 * math.exp(-0.3 * 0)

LANES = 128
IN_TILE = 512
OUT_TILE = 256
Q_TILE = 256
DFT_ROWS = 512
MIB = 1024 * 1024

BF16 = jnp.bfloat16
F32 = jnp.float32


def _dot(a, b):
    return jnp.dot(a, b, preferred_element_type=F32)


def _rmsnorm_rows(x, g):
    ms = jnp.mean(x * x, axis=-1, keepdims=True)
    return x * lax.rsqrt(ms + EPS) * g


def _resident(shape):
    return pl.BlockSpec(shape, lambda *_: (0,) * len(shape), pipeline_mode=pl.Buffered(1))


def _qk_head_norm_rope(z, gain, cos, sin_signed, first_map, low_half):
    zz = z * z
    s1 = jnp.sum(jnp.where(first_map, zz, 0.0), axis=-1, keepdims=True)
    s2 = jnp.sum(jnp.where(first_map, 0.0, zz), axis=-1, keepdims=True)
    inv = jnp.where(first_map,
                    lax.rsqrt(s1 * (1.0 / HEAD_DIM) + EPS),
                    lax.rsqrt(s2 * (1.0 / HEAD_DIM) + EPS))
    zn = z * inv * gain
    half = HEAD_DIM // 2
    rot = jnp.where(low_half, pltpu.roll(zn, LANES - half, 1), pltpu.roll(zn, half, 1))
    return zn * cos + rot * sin_signed


def _inproj_kernel(x_ref, gmix_ref, w_ref, gq_ref, gk_ref, cos_ref, sin_ref, dc_ref,
                   q_ref, k_ref, v_ref, y1_ref, y2_ref):
    h = _rmsnorm_rows(x_ref[...], gmix_ref[...]).astype(BF16)
    lane = lax.broadcasted_iota(jnp.int32, (1, LANES), 1)
    first_map = lane < HEAD_DIM
    low_half = (lane % HEAD_DIM) < (HEAD_DIM // 2)
    cos = cos_ref[...]
    sin_signed = sin_ref[...]

    for off, g_ref, o_ref in ((0, gq_ref, q_ref), (QK_WIDTH, gk_ref, k_ref)):
        z = _dot(h, w_ref[:, off:off + QK_WIDTH])
        gain = g_ref[...]
        for hh in range(N_HEADS):
            sl = slice(hh * LANES, (hh + 1) * LANES)
            o_ref[:, sl] = _qk_head_norm_rope(z[:, sl], gain, cos, sin_signed,
                                              first_map, low_half).astype(BF16)

    v_ref[...] = _dot(h, w_ref[:, 2 * QK_WIDTH:2 * QK_WIDTH + V_WIDTH]).astype(BF16)

    f = _dot(h, w_ref[:, 2 * QK_WIDTH + V_WIDTH:IN_WIDTH]).astype(BF16)
    dc = dc_ref[...]
    for g in range(N_FOURIER_GROUPS):
        sl = slice(g * FOURIER_GROUP, (g + 1) * FOURIER_GROUP)
        y = _dot(f[:, sl], dc)
        y1_ref[:, sl] = y[:, :FOURIER_GROUP].astype(BF16)
        y2_ref[:, sl] = y[:, FOURIER_GROUP:].astype(BF16)


def _inproj(x, gmix, w_in, gq, gk, cos, sin_signed, dc):
    n_tok = x.shape[0]
    tm = IN_TILE
    pos_blocks = SEQ // tm
    tok = lambda w: pl.BlockSpec((tm, w), lambda i: (i, 0))
    rope = pl.BlockSpec((tm, LANES), lambda i: (i % pos_blocks, 0))
    out_shapes = (
        jax.ShapeDtypeStruct((n_tok, QK_WIDTH), BF16),
        jax.ShapeDtypeStruct((n_tok, QK_WIDTH), BF16),
        jax.ShapeDtypeStruct((n_tok, V_WIDTH), BF16),
        jax.ShapeDtypeStruct((n_tok, FOURIER_WIDTH), BF16),
        jax.ShapeDtypeStruct((n_tok, FOURIER_WIDTH), BF16),
    )
    return pl.pallas_call(
        _inproj_kernel,
        grid=(n_tok // tm,),
        in_specs=[tok(D_MODEL), _resident((1, D_MODEL)), _resident((D_MODEL, IN_WIDTH)),
                  _resident((1, LANES)), _resident((1, LANES)), rope, rope,
                  _resident((FOURIER_GROUP, 2 * FOURIER_GROUP))],
        out_specs=(tok(QK_WIDTH), tok(QK_WIDTH), tok(V_WIDTH), tok(FOURIER_WIDTH),
                   tok(FOURIER_WIDTH)),
        out_shape=out_shapes,
        compiler_params=pltpu.CompilerParams(
            dimension_semantics=("arbitrary",), vmem_limit_bytes=48 * MIB),
        name="inproj",
    )(x, gmix, w_in, gq, gk, cos, sin_signed, dc)


def _attn_kernel(lq1_ref, lk1_ref, lq2_ref, lk2_ref, gsub_ref, q_ref, k_ref, v_ref,
                 o_ref, lam_ref, s_even, s_odd, e_ref):
    @pl.when((pl.program_id(0) == 0) & (pl.program_id(1) == 0))
    def _():
        a1 = jnp.sum(lq1_ref[...] * lk1_ref[...], axis=-1, keepdims=True)
        a2 = jnp.sum(lq2_ref[...] * lk2_ref[...], axis=-1, keepdims=True)
        lam_ref[...] = jnp.exp(a1) - jnp.exp(a2) + LAMBDA_INIT

    lam = lam_ref[...]
    k = k_ref[0]
    v_t = v_ref[0].T
    gsub = gsub_ref[...]
    lane = lax.broadcasted_iota(jnp.int32, (1, LANES), 1)
    first_map = lane < HEAD_DIM
    tq = Q_TILE
    n_blocks = SEQ // tq

    def scores(i, s_ref):
        q = q_ref[0, pl.ds(pl.multiple_of(i * tq, tq), tq), :]
        zero = jnp.zeros_like(q)
        qq = jnp.concatenate([jnp.where(first_map, q, zero), jnp.where(first_map, zero, q)],
                             axis=0)
        s_t = lax.dot_general(k, qq, (((1,), (1,)), ((), ())), preferred_element_type=F32)
        s_ref[...] = s_t
        return jnp.max(s_t, axis=0, keepdims=True)

    def softmax_pv(i, s_ref, m):
        inv = []
        for c in range(2 * tq // LANES):
            cols = slice(c * LANES, (c + 1) * LANES)
            e = jnp.exp2(s_ref[:, cols] - m[:, cols])
            inv.append(1.0 / jnp.sum(e, axis=0, keepdims=True))
            e_ref[c * LANES // tq, :, (c * LANES) % tq:(c * LANES) % tq + LANES] = e.astype(BF16)
        half = tq // LANES
        r1 = jnp.concatenate(inv[:half], axis=1).astype(BF16)
        r2 = (lam * jnp.concatenate(inv[half:], axis=1)).astype(BF16)
        a_t = e_ref[0] * r1 - e_ref[1] * r2
        o_t = _dot(v_t, a_t)
        ms = jnp.mean(o_t * o_t, axis=0, keepdims=True)
        o_n = o_t * lax.rsqrt(ms + EPS) * gsub
        o_ref[0, pl.ds(pl.multiple_of(i * tq, tq), tq), :] = o_n.T.astype(BF16)

    def pair(j, m_even):
        m_odd = scores(2 * j + 1, s_odd)
        softmax_pv(2 * j, s_even, m_even)
        m_next = scores(2 * j + 2, s_even)
        softmax_pv(2 * j + 1, s_odd, m_odd)
        return m_next

    m_even = lax.fori_loop(0, n_blocks // 2 - 1, pair, scores(0, s_even))
    m_odd = scores(n_blocks - 1, s_odd)
    softmax_pv(n_blocks - 2, s_even, m_even)
    softmax_pv(n_blocks - 1, s_odd, m_odd)


def _attention(lq1, lk1, lq2, lk2, gsub_t, q, k, v):
    n_seq = q.shape[0]
    head = pl.BlockSpec((1, SEQ, LANES), lambda b, h: (b, 0, h))
    return pl.pallas_call(
        _attn_kernel,
        grid=(n_seq, N_HEADS),
        in_specs=[_resident((1, HEAD_DIM))] * 4 + [_resident((V_HEAD_DIM, Q_TILE)),
                                                    head, head, head],
        out_specs=head,
        out_shape=jax.ShapeDtypeStruct((n_seq, SEQ, V_WIDTH), BF16),
        scratch_shapes=[pltpu.VMEM((1, 1), F32),
                        pltpu.VMEM((SEQ, 2 * Q_TILE), F32),
                        pltpu.VMEM((SEQ, 2 * Q_TILE), F32),
                        pltpu.VMEM((2, SEQ, Q_TILE), BF16)],
        compiler_params=pltpu.CompilerParams(
            dimension_semantics=("arbitrary", "arbitrary"), vmem_limit_bytes=40 * MIB),
        name="diff_attn",
    )(lq1, lk1, lq2, lk2, gsub_t, q, k, v)


def _seq_dft_kernel(cs_ref, nsn_ref, y1_ref, y2_ref, o_ref):
    y1 = y1_ref[0]
    y2 = y2_ref[0]

    def body(i, carry):
        rows = pl.ds(pl.multiple_of(i * DFT_ROWS, DFT_ROWS), DFT_ROWS)
        acc = _dot(cs_ref[rows, :], y1) + _dot(nsn_ref[rows, :], y2)
        o_ref[0, rows, :] = acc.astype(BF16)
        return carry

    lax.fori_loop(0, SEQ // DFT_ROWS, body, 0)


def _seq_dft(cs, nsn, y1, y2):
    n_seq = y1.shape[0]
    seq = pl.BlockSpec((1, SEQ, FOURIER_WIDTH), lambda b: (b, 0, 0))
    return pl.pallas_call(
        _seq_dft_kernel,
        grid=(n_seq,),
        in_specs=[_resident((SEQ, SEQ)), _resident((SEQ, SEQ)), seq, seq],
        out_specs=seq,
        out_shape=jax.ShapeDtypeStruct((n_seq, SEQ, FOURIER_WIDTH), BF16),
        compiler_params=pltpu.CompilerParams(
            dimension_semantics=("arbitrary",), vmem_limit_bytes=48 * MIB),
        name="seq_dft",
    )(cs, nsn, y1, y2)


def _merge_mlp_kernel(x_ref, att_ref, four_ref, gmix_ref, wgate_ref, bgate_ref, wattn_ref,
                      wfour_ref, wout_ref, gmlp_ref, wup_ref, wdown_ref, y_ref):
    x = x_ref[...]
    h = _rmsnorm_rows(x, gmix_ref[...]).astype(BF16)
    gates = jax.nn.sigmoid(_dot(h, wgate_ref[...]) + bgate_ref[...])
    attn_out = _dot(att_ref[...], wattn_ref[...])
    four_out = _dot(four_ref[...], wfour_ref[...])
    mixed = gates[:, :D_MODEL] * attn_out + gates[:, D_MODEL:] * four_out
    x1 = x + _dot(mixed.astype(BF16), wout_ref[...])
    h2 = _rmsnorm_rows(x1, gmlp_ref[...]).astype(BF16)
    u = jnp.square(jnp.maximum(_dot(h2, wup_ref[...]), 0.0)).astype(BF16)
    y_ref[...] = x1 + _dot(u, wdown_ref[...])


def _merge_mlp(x, att, four, gmix, wgate, bgate, wattn, wfour, wout, gmlp, wup, wdown):
    n_tok = x.shape[0]
    tm = OUT_TILE
    tok = lambda w: pl.BlockSpec((tm, w), lambda i: (i, 0))
    return pl.pallas_call(
        _merge_mlp_kernel,
        grid=(n_tok // tm,),
        in_specs=[tok(D_MODEL), tok(V_WIDTH), tok(FOURIER_WIDTH), _resident((1, D_MODEL)),
                  _resident((D_MODEL, 2 * D_MODEL)), _resident((1, 2 * D_MODEL)),
                  _resident((V_WIDTH, D_MODEL)), _resident((FOURIER_WIDTH, D_MODEL)),
                  _resident((D_MODEL, D_MODEL)), _resident((1, D_MODEL)),
                  _resident((D_MODEL, D_FF)), _resident((D_FF, D_MODEL))],
        out_specs=tok(D_MODEL),
        out_shape=jax.ShapeDtypeStruct((n_tok, D_MODEL), F32),
        compiler_params=pltpu.CompilerParams(
            dimension_semantics=("arbitrary",), vmem_limit_bytes=56 * MIB),
        name="merge_mlp",
    )(x, att, four, gmix, wgate, bgate, wattn, wfour, wout, gmlp, wup, wdown)


def _rope_tables():
    half = HEAD_DIM // 2
    freqs = ROPE_THETA ** (-jnp.arange(half, dtype=F32) * 2.0 / HEAD_DIM)
    ang = jnp.arange(SEQ, dtype=F32)[:, None] * freqs[None, :]
    cos = jnp.tile(jnp.cos(ang), (1, LANES // half))
    sin = jnp.tile(jnp.sin(ang), (1, LANES // half))
    lane = jnp.arange(LANES)
    sign = jnp.where((lane % HEAD_DIM) < half, -1.0, 1.0).astype(F32)
    return cos, sin * sign[None, :]


def _dft_angles(n):
    idx = jnp.arange(n, dtype=jnp.int32)
    kn = (idx[:, None] * idx[None, :]) % n
    return kn.astype(F32) * (2.0 * math.pi / n)


def _dft_tables():
    norm = 1.0 / math.sqrt(SEQ * FOURIER_GROUP)
    ang_s = _dft_angles(SEQ)
    cs = (jnp.cos(ang_s) * norm).astype(BF16)
    nsn = (jnp.sin(ang_s) * (-norm)).astype(BF16)
    ang_c = _dft_angles(FOURIER_GROUP)
    dc = jnp.concatenate([jnp.cos(ang_c), jnp.sin(ang_c)], axis=1).astype(BF16)
    return cs, nsn, dc


def kernel(x_prompt, x_sample, g_mix, w_in, g_q, g_k, lam_q1, lam_k1, lam_q2, lam_k2, g_sub,
           w_attn_br, w_four_br, w_gate, b_gate, w_out, g_mlp, w_up, w_down):
    row = lambda p: p[0].reshape(1, -1).astype(F32)
    as_bf16 = lambda w: w[0].astype(BF16)
    cos, sin_signed = _rope_tables()
    cs, nsn, dc = _dft_tables()
    gq = jnp.tile(row(g_q), (1, LANES // HEAD_DIM)) * (HEAD_DIM ** -0.5 * math.log2(math.e))
    gsub_t = jnp.broadcast_to(g_sub[0].astype(F32)[:, None] * (1.0 - LAMBDA_INIT),
                              (V_HEAD_DIM, Q_TILE))
    gk = jnp.tile(row(g_k), (1, LANES // HEAD_DIM))
    gmix = row(g_mix)
    lams = (row(lam_q1), row(lam_k1), row(lam_q2), row(lam_k2))
    w_in_b, w_gate_b, w_attn_b, w_four_b, w_out_b, w_up_b, w_down_b = map(
        as_bf16, (w_in, w_gate, w_attn_br, w_four_br, w_out, w_up, w_down))

    def layer(x3):
        n_seq = x3.shape[0]
        x = x3.reshape(n_seq * SEQ, D_MODEL)
        q, k, v, y1, y2 = _inproj(x, gmix, w_in_b, gq, gk, cos, sin_signed, dc)
        seq3 = lambda a: a.reshape(n_seq, SEQ, a.shape[-1])
        att = _attention(*lams, gsub_t, seq3(q), seq3(k), seq3(v))
        four = _seq_dft(cs, nsn, seq3(y1), seq3(y2))
        y = _merge_mlp(x, att.reshape(n_seq * SEQ, V_WIDTH),
                       four.reshape(n_seq * SEQ, FOURIER_WIDTH), gmix, w_gate_b, row(b_gate),
                       w_attn_b, w_four_b, w_out_b, row(g_mlp), w_up_b, w_down_b)
        return y.reshape(n_seq, SEQ, D_MODEL)

    return layer(x_prompt), layer(x_sample)
```

```python
import functools
import math

import jax
import jax.numpy as jnp
from jax import lax
from jax.experimental import pallas as pl
from jax.experimental.pallas import tpu as pltpu

D_MODEL = 1024
SEQ = 2048
N_HEADS = 8
HEAD_DIM = 64
V_HEAD_DIM = 2 * HEAD_DIM
QK_WIDTH = 2 * N_HEADS * HEAD_DIM
V_WIDTH = N_HEADS * V_HEAD_DIM
N_FOURIER_GROUPS = 4
FOURIER_GROUP = 128
FOURIER_WIDTH = N_FOURIER_GROUPS * FOURIER_GROUP
IN_WIDTH = 2 * QK_WIDTH + V_WIDTH + FOURIER_WIDTH
D_FF = 4 * D_MODEL
ROPE_THETA = 10000.0
EPS = 1e-6
LAMBDA_INIT = 0.8 - 0.6 * math.exp(-0.3 * 0)

LANES = 128
IN_TILE = 512
OUT_TILE = 256
Q_TILE = 256
DFT_ROWS = 512
ONES_ROWS = 16
SCORE_BOUND_LIMIT = 48.0
MIB = 1024 * 1024

BF16 = jnp.bfloat16
F32 = jnp.float32


def _dot(a, b):
    return jnp.dot(a, b, preferred_element_type=F32)


def _rmsnorm_rows(x, g):
    ms = jnp.mean(x * x, axis=-1, keepdims=True)
    return x * lax.rsqrt(ms + EPS) * g


def _resident(shape):
    return pl.BlockSpec(shape, lambda *_: (0,) * len(shape), pipeline_mode=pl.Buffered(1))


def _qk_head_norm_rope(z, gain, cos, sin_signed, first_map, low_half):
    zz = z * z
    s1 = jnp.sum(jnp.where(first_map, zz, 0.0), axis=-1, keepdims=True)
    s2 = jnp.sum(jnp.where(first_map, 0.0, zz), axis=-1, keepdims=True)
    inv = jnp.where(first_map,
                    lax.rsqrt(s1 * (1.0 / HEAD_DIM) + EPS),
                    lax.rsqrt(s2 * (1.0 / HEAD_DIM) + EPS))
    zn = z * inv * gain
    half = HEAD_DIM // 2
    rot = jnp.where(low_half, pltpu.roll(zn, LANES - half, 1), pltpu.roll(zn, half, 1))
    return zn * cos + rot * sin_signed


def _inproj_kernel(x_ref, gmix_ref, w_ref, gq_ref, gk_ref, cos_ref, sin_ref, dc_ref,
                   q_ref, k_ref, v_ref, y1_ref, y2_ref):
    h = _rmsnorm_rows(x_ref[...], gmix_ref[...]).astype(BF16)
    lane = lax.broadcasted_iota(jnp.int32, (1, LANES), 1)
    first_map = lane < HEAD_DIM
    low_half = (lane % HEAD_DIM) < (HEAD_DIM // 2)
    cos = cos_ref[...]
    sin_signed = sin_ref[...]

    for off, g_ref, o_ref in ((0, gq_ref, q_ref), (QK_WIDTH, gk_ref, k_ref)):
        z = _dot(h, w_ref[:, off:off + QK_WIDTH])
        gain = g_ref[...]
        for hh in range(N_HEADS):
            sl = slice(hh * LANES, (hh + 1) * LANES)
            o_ref[:, sl] = _qk_head_norm_rope(z[:, sl], gain, cos, sin_signed,
                                              first_map, low_half).astype(BF16)

    v_ref[...] = _dot(h, w_ref[:, 2 * QK_WIDTH:2 * QK_WIDTH + V_WIDTH]).astype(BF16)

    f = _dot(h, w_ref[:, 2 * QK_WIDTH + V_WIDTH:IN_WIDTH]).astype(BF16)
    dc = dc_ref[...]
    for g in range(N_FOURIER_GROUPS):
        sl = slice(g * FOURIER_GROUP, (g + 1) * FOURIER_GROUP)
        y = _dot(f[:, sl], dc)
        y1_ref[:, sl] = y[:, :FOURIER_GROUP].astype(BF16)
        y2_ref[:, sl] = y[:, FOURIER_GROUP:].astype(BF16)


def _inproj(x, gmix, w_in, gq, gk, cos, sin_signed, dc):
    n_tok = x.shape[0]
    tm = IN_TILE
    pos_blocks = SEQ // tm
    tok = lambda w: pl.BlockSpec((tm, w), lambda i: (i, 0))
    rope = pl.BlockSpec((tm, LANES), lambda i: (i % pos_blocks, 0))
    out_shapes = (
        jax.ShapeDtypeStruct((n_tok, QK_WIDTH), BF16),
        jax.ShapeDtypeStruct((n_tok, QK_WIDTH), BF16),
        jax.ShapeDtypeStruct((n_tok, V_WIDTH), BF16),
        jax.ShapeDtypeStruct((n_tok, FOURIER_WIDTH), BF16),
        jax.ShapeDtypeStruct((n_tok, FOURIER_WIDTH), BF16),
    )
    return pl.pallas_call(
        _inproj_kernel,
        grid=(n_tok // tm,),
        in_specs=[tok(D_MODEL), _resident((1, D_MODEL)), _resident((D_MODEL, IN_WIDTH)),
                  _resident((1, LANES)), _resident((1, LANES)), rope, rope,
                  _resident((FOURIER_GROUP, 2 * FOURIER_GROUP))],
        out_specs=(tok(QK_WIDTH), tok(QK_WIDTH), tok(V_WIDTH), tok(FOURIER_WIDTH),
                   tok(FOURIER_WIDTH)),
        out_shape=out_shapes,
        compiler_params=pltpu.CompilerParams(
            dimension_semantics=("arbitrary",), vmem_limit_bytes=48 * MIB),
        name="inproj",
    )(x, gmix, w_in, gq, gk, cos, sin_signed, dc)


def _attn_kernel(bound_ref, lq1_ref, lk1_ref, lq2_ref, lk2_ref, gsub_ref, q_ref, k_ref, v_ref,
                 o_ref, lam_ref, vaug_ref):
    @pl.when((pl.program_id(0) == 0) & (pl.program_id(1) == 0))
    def _():
        a1 = jnp.sum(lq1_ref[...] * lk1_ref[...], axis=-1, keepdims=True)
        a2 = jnp.sum(lq2_ref[...] * lk2_ref[...], axis=-1, keepdims=True)
        lam_ref[...] = jnp.exp(a1) - jnp.exp(a2) + LAMBDA_INIT

    lam = lam_ref[...]
    k = k_ref[0]
    vaug_ref[:V_HEAD_DIM, :] = v_ref[0].T
    ones_row = lax.broadcasted_iota(jnp.int32, (ONES_ROWS, SEQ), 0) == 0
    vaug_ref[V_HEAD_DIM:, :] = jnp.where(ones_row, 1.0, 0.0).astype(BF16)
    v_aug = vaug_ref[...]
    gsub = gsub_ref[...]
    lane = lax.broadcasted_iota(jnp.int32, (1, LANES), 1)
    first_map = lane < HEAD_DIM
    tq = Q_TILE

    def block(i, shift_by_max):
        rows = pl.ds(pl.multiple_of(i * tq, tq), tq)
        q = q_ref[0, rows, :]
        zero = jnp.zeros_like(q)
        qq = jnp.concatenate([jnp.where(first_map, q, zero), jnp.where(first_map, zero, q)],
                             axis=0)
        s_t = lax.dot_general(k, qq, (((1,), (1,)), ((), ())), preferred_element_type=F32)
        if shift_by_max:
            s_t = s_t - jnp.max(s_t, axis=0, keepdims=True)
        e = jnp.exp2(s_t).astype(BF16)
        acc = _dot(v_aug, e)
        inv = 1.0 / acc[V_HEAD_DIM:V_HEAD_DIM + 1, :]
        o_t = acc[:V_HEAD_DIM, :tq] * inv[:, :tq] - acc[:V_HEAD_DIM, tq:] * (lam * inv[:, tq:])
        ms = jnp.mean(o_t * o_t, axis=0, keepdims=True)
        o_n = o_t * lax.rsqrt(ms + EPS) * gsub
        o_ref[0, rows, :] = o_n.T.astype(BF16)

    def all_blocks(shift_by_max):
        def body(i, carry):
            block(i, shift_by_max)
            return carry
        lax.fori_loop(0, SEQ // tq, body, 0, unroll=SEQ // tq if not shift_by_max else 1)

    no_shift_needed = bound_ref[0] <= SCORE_BOUND_LIMIT
    pl.when(no_shift_needed)(functools.partial(all_blocks, False))
    pl.when(jnp.logical_not(no_shift_needed))(functools.partial(all_blocks, True))


def _attention(bound, lq1, lk1, lq2, lk2, gsub_t, q, k, v):
    n_seq = q.shape[0]
    head = pl.BlockSpec((1, SEQ, LANES), lambda b, h: (b, 0, h))
    return pl.pallas_call(
        _attn_kernel,
        grid=(n_seq, N_HEADS),
        in_specs=[pl.BlockSpec(memory_space=pltpu.SMEM)] + [_resident((1, HEAD_DIM))] * 4
        + [_resident((V_HEAD_DIM, Q_TILE)), head, head, head],
        out_specs=head,
        out_shape=jax.ShapeDtypeStruct((n_seq, SEQ, V_WIDTH), BF16),
        scratch_shapes=[pltpu.VMEM((1, 1), F32),
                        pltpu.VMEM((V_HEAD_DIM + ONES_ROWS, SEQ), BF16)],
        compiler_params=pltpu.CompilerParams(
            dimension_semantics=("arbitrary", "arbitrary"), vmem_limit_bytes=40 * MIB),
        name="diff_attn",
    )(bound, lq1, lk1, lq2, lk2, gsub_t, q, k, v)


def _seq_dft_kernel(cs_ref, nsn_ref, y1_ref, y2_ref, o_ref):
    y1 = y1_ref[0]
    y2 = y2_ref[0]

    def body(i, carry):
        rows = pl.ds(pl.multiple_of(i * DFT_ROWS, DFT_ROWS), DFT_ROWS)
        acc = _dot(cs_ref[rows, :], y1) + _dot(nsn_ref[rows, :], y2)
        o_ref[0, rows, :] = acc.astype(BF16)
        return carry

    lax.fori_loop(0, SEQ // DFT_ROWS, body, 0)


def _seq_dft(cs, nsn, y1, y2):
    n_seq = y1.shape[0]
    seq = pl.BlockSpec((1, SEQ, FOURIER_WIDTH), lambda b: (b, 0, 0))
    return pl.pallas_call(
        _seq_dft_kernel,
        grid=(n_seq,),
        in_specs=[_resident((SEQ, SEQ)), _resident((SEQ, SEQ)), seq, seq],
        out_specs=seq,
        out_shape=jax.ShapeDtypeStruct((n_seq, SEQ, FOURIER_WIDTH), BF16),
        compiler_params=pltpu.CompilerParams(
            dimension_semantics=("arbitrary",), vmem_limit_bytes=48 * MIB),
        name="seq_dft",
    )(cs, nsn, y1, y2)


def _merge_mlp_kernel(x_ref, att_ref, four_ref, gmix_ref, wgate_ref, bgate_ref, wattn_ref,
                      wfour_ref, wout_ref, gmlp_ref, wup_ref, wdown_ref, y_ref):
    x = x_ref[...]
    h = _rmsnorm_rows(x, gmix_ref[...]).astype(BF16)
    gates = jax.nn.sigmoid(_dot(h, wgate_ref[...]) + bgate_ref[...])
    attn_out = _dot(att_ref[...], wattn_ref[...])
    four_out = _dot(four_ref[...], wfour_ref[...])
    mixed = gates[:, :D_MODEL] * attn_out + gates[:, D_MODEL:] * four_out
    x1 = x + _dot(mixed.astype(BF16), wout_ref[...])
    h2 = _rmsnorm_rows(x1, gmlp_ref[...]).astype(BF16)
    u = jnp.square(jnp.maximum(_dot(h2, wup_ref[...]), 0.0)).astype(BF16)
    y_ref[...] = x1 + _dot(u, wdown_ref[...])


def _merge_mlp(x, att, four, gmix, wgate, bgate, wattn, wfour, wout, gmlp, wup, wdown):
    n_tok = x.shape[0]
    tm = OUT_TILE
    tok = lambda w: pl.BlockSpec((tm, w), lambda i: (i, 0))
    return pl.pallas_call(
        _merge_mlp_kernel,
        grid=(n_tok // tm,),
        in_specs=[tok(D_MODEL), tok(V_WIDTH), tok(FOURIER_WIDTH), _resident((1, D_MODEL)),
                  _resident((D_MODEL, 2 * D_MODEL)), _resident((1, 2 * D_MODEL)),
                  _resident((V_WIDTH, D_MODEL)), _resident((FOURIER_WIDTH, D_MODEL)),
                  _resident((D_MODEL, D_MODEL)), _resident((1, D_MODEL)),
                  _resident((D_MODEL, D_FF)), _resident((D_FF, D_MODEL))],
        out_specs=tok(D_MODEL),
        out_shape=jax.ShapeDtypeStruct((n_tok, D_MODEL), F32),
        compiler_params=pltpu.CompilerParams(
            dimension_semantics=("arbitrary",), vmem_limit_bytes=56 * MIB),
        name="merge_mlp",
    )(x, att, four, gmix, wgate, bgate, wattn, wfour, wout, gmlp, wup, wdown)


def _rope_tables():
    half = HEAD_DIM // 2
    freqs = ROPE_THETA ** (-jnp.arange(half, dtype=F32) * 2.0 / HEAD_DIM)
    ang = jnp.arange(SEQ, dtype=F32)[:, None] * freqs[None, :]
    cos = jnp.tile(jnp.cos(ang), (1, LANES // half))
    sin = jnp.tile(jnp.sin(ang), (1, LANES // half))
    lane = jnp.arange(LANES)
    sign = jnp.where((lane % HEAD_DIM) < half, -1.0, 1.0).astype(F32)
    return cos, sin * sign[None, :]


def _dft_angles(n):
    idx = jnp.arange(n, dtype=jnp.int32)
    kn = (idx[:, None] * idx[None, :]) % n
    return kn.astype(F32) * (2.0 * math.pi / n)


def _seq_dft_tables(norm):
    k = jnp.arange(SEQ, dtype=jnp.int32)[:, None]
    step = 2.0 * math.pi / SEQ
    ang_a = ((k * (jnp.arange(SEQ // LANES, dtype=jnp.int32) * LANES)[None, :]) % SEQ).astype(F32) * step
    ang_b = ((k * jnp.arange(LANES, dtype=jnp.int32)[None, :]) % SEQ).astype(F32) * step
    ca, sa = jnp.cos(ang_a)[:, :, None] * norm, jnp.sin(ang_a)[:, :, None] * norm
    cb, sb = jnp.cos(ang_b)[:, None, :], jnp.sin(ang_b)[:, None, :]
    cs = (ca * cb - sa * sb).reshape(SEQ, SEQ)
    nsn = (-(sa * cb + ca * sb)).reshape(SEQ, SEQ)
    return cs.astype(BF16), nsn.astype(BF16)


def _dft_tables():
    norm = 1.0 / math.sqrt(SEQ * FOURIER_GROUP)
    cs, nsn = _seq_dft_tables(norm)
    ang_c = _dft_angles(FOURIER_GROUP)
    dc = jnp.concatenate([jnp.cos(ang_c), jnp.sin(ang_c)], axis=1).astype(BF16)
    return cs, nsn, dc


def kernel(x_prompt, x_sample, g_mix, w_in, g_q, g_k, lam_q1, lam_k1, lam_q2, lam_k2, g_sub,
           w_attn_br, w_four_br, w_gate, b_gate, w_out, g_mlp, w_up, w_down):
    row = lambda p: p[0].reshape(1, -1).astype(F32)
    as_bf16 = lambda w: w[0].astype(BF16)
    cos, sin_signed = _rope_tables()
    cs, nsn, dc = _dft_tables()
    gq = jnp.tile(row(g_q), (1, LANES // HEAD_DIM)) * (HEAD_DIM ** -0.5 * math.log2(math.e))
    gk = jnp.tile(row(g_k), (1, LANES // HEAD_DIM))
    score_bound = (HEAD_DIM * jnp.max(jnp.abs(gq)) * jnp.max(jnp.abs(gk))).reshape(1)
    gsub_t = jnp.broadcast_to(g_sub[0].astype(F32)[:, None] * (1.0 - LAMBDA_INIT),
                              (V_HEAD_DIM, Q_TILE))
    gmix = row(g_mix)
    lams = (row(lam_q1), row(lam_k1), row(lam_q2), row(lam_k2))
    w_in_b, w_gate_b, w_attn_b, w_four_b, w_out_b, w_up_b, w_down_b = map(
        as_bf16, (w_in, w_gate, w_attn_br, w_four_br, w_out, w_up, w_down))

    def layer(x3):
        n_seq = x3.shape[0]
        x = x3.reshape(n_seq * SEQ, D_MODEL)
        q, k, v, y1, y2 = _inproj(x, gmix, w_in_b, gq, gk, cos, sin_signed, dc)
        seq3 = lambda a: a.reshape(n_seq, SEQ, a.shape[-1])
        att = _attention(score_bound, *lams, gsub_t, seq3(q), seq3(k), seq3(v))
        four = _seq_dft(cs, nsn, seq3(y1), seq3(y2))
        y = _merge_mlp(x, att.reshape(n_seq * SEQ, V_WIDTH),
                       four.reshape(n_seq * SEQ, FOURIER_WIDTH), gmix, w_gate_b, row(b_gate),
                       w_attn_b, w_four_b, w_out_b, row(g_mlp), w_up_b, w_down_b)
        return y.reshape(n_seq, SEQ, D_MODEL)

    return layer(x_prompt), layer(x_sample)
```

```python
import functools
import math

import jax
import jax.numpy as jnp
from jax import lax
from jax.experimental import pallas as pl
from jax.experimental.pallas import tpu as pltpu

D_MODEL = 1024
SEQ = 2048
N_HEADS = 8
HEAD_DIM = 64
V_HEAD_DIM = 2 * HEAD_DIM
QK_WIDTH = 2 * N_HEADS * HEAD_DIM
V_WIDTH = N_HEADS * V_HEAD_DIM
N_FOURIER_GROUPS = 4
FOURIER_GROUP = 128
FOURIER_WIDTH = N_FOURIER_GROUPS * FOURIER_GROUP
IN_WIDTH = 2 * QK_WIDTH + V_WIDTH + FOURIER_WIDTH
D_FF = 4 * D_MODEL
ROPE_THETA = 10000.0
EPS = 1e-6
LAMBDA_INIT = 0.8 - 0.6 * math.exp(-0.3 * 0)

LANES = 128
IN_TILE = 512
IN_ROWS = 512
OUT_TILE = 256
Q_TILE = 256
DFT_ROWS = 512
ONES_ROWS = 16
SCORE_BOUND_LIMIT = 48.0
MIB = 1024 * 1024

BF16 = jnp.bfloat16
F32 = jnp.float32


def _dot(a, b):
    return jnp.dot(a, b, preferred_element_type=F32)


def _rmsnorm_rows(x, g):
    ms = jnp.mean(x * x, axis=-1, keepdims=True)
    return x * lax.rsqrt(ms + EPS) * g


def _resident(shape):
    return pl.BlockSpec(shape, lambda *_: (0,) * len(shape), pipeline_mode=pl.Buffered(1))


def _map_norm_rope_t(z, gain, cos, sin):
    ss = jnp.sum(z * z, axis=0, keepdims=True)
    zn = z * lax.rsqrt(ss * (1.0 / HEAD_DIM) + EPS) * gain
    half = HEAD_DIM // 2
    x1, x2 = zn[:half], zn[half:]
    return x1 * cos - x2 * sin, x2 * cos + x1 * sin


def _inproj_kernel(x_ref, gmix_ref, wt_ref, wf_ref, gq_ref, gk_ref, cos_ref, sin_ref, dc_ref,
                   qt_ref, kt_ref, vt_ref, y1_ref, y2_ref):
    h = _rmsnorm_rows(x_ref[...], gmix_ref[...]).astype(BF16)
    h_t = h.T
    cos = cos_ref[...]
    sin = sin_ref[...]
    half = HEAD_DIM // 2

    for off, g_ref, o_ref in ((0, gq_ref, qt_ref), (QK_WIDTH, gk_ref, kt_ref)):
        gain = g_ref[...]
        for c in range(QK_WIDTH // IN_ROWS):
            z_t = _dot(wt_ref[off + c * IN_ROWS:off + (c + 1) * IN_ROWS, :], h_t)
            for m in range(IN_ROWS // HEAD_DIM):
                lo, hi = _map_norm_rope_t(z_t[m * HEAD_DIM:(m + 1) * HEAD_DIM], gain, cos, sin)
                r0 = c * IN_ROWS + m * HEAD_DIM
                o_ref[0, r0:r0 + half, :] = lo.astype(BF16)
                o_ref[0, r0 + half:r0 + HEAD_DIM, :] = hi.astype(BF16)

    for c in range(V_WIDTH // IN_ROWS):
        rows = slice(c * IN_ROWS, (c + 1) * IN_ROWS)
        vt_ref[0, rows, :] = _dot(wt_ref[2 * QK_WIDTH + c * IN_ROWS:2 * QK_WIDTH + (c + 1) * IN_ROWS, :],
                                  h_t).astype(BF16)

    f = _dot(h, wf_ref[...]).astype(BF16)
    dc = dc_ref[...]
    for g in range(N_FOURIER_GROUPS):
        sl = slice(g * FOURIER_GROUP, (g + 1) * FOURIER_GROUP)
        y = _dot(f[:, sl], dc)
        y1_ref[:, sl] = y[:, :FOURIER_GROUP].astype(BF16)
        y2_ref[:, sl] = y[:, FOURIER_GROUP:].astype(BF16)


def _inproj(x, gmix, w_qkv_t, w_f, gq, gk, cos_t, sin_t, dc):
    n_tok = x.shape[0]
    n_seq = n_tok // SEQ
    tm = IN_TILE
    pos_blocks = SEQ // tm
    tok = lambda w: pl.BlockSpec((tm, w), lambda i: (i, 0))
    rope = pl.BlockSpec((HEAD_DIM // 2, tm), lambda i: (0, i % pos_blocks))
    dims_by_tok = lambda w: pl.BlockSpec((1, w, tm), lambda i: (i // pos_blocks, 0, i % pos_blocks))
    out_shapes = (
        jax.ShapeDtypeStruct((n_seq, QK_WIDTH, SEQ), BF16),
        jax.ShapeDtypeStruct((n_seq, QK_WIDTH, SEQ), BF16),
        jax.ShapeDtypeStruct((n_seq, V_WIDTH, SEQ), BF16),
        jax.ShapeDtypeStruct((n_tok, FOURIER_WIDTH), BF16),
        jax.ShapeDtypeStruct((n_tok, FOURIER_WIDTH), BF16),
    )
    return pl.pallas_call(
        _inproj_kernel,
        grid=(n_tok // tm,),
        in_specs=[tok(D_MODEL), _resident((1, D_MODEL)),
                  _resident((2 * QK_WIDTH + V_WIDTH, D_MODEL)), _resident((D_MODEL, FOURIER_WIDTH)),
                  _resident((HEAD_DIM, tm)), _resident((HEAD_DIM, tm)), rope, rope,
                  _resident((FOURIER_GROUP, 2 * FOURIER_GROUP))],
        out_specs=(dims_by_tok(QK_WIDTH), dims_by_tok(QK_WIDTH), dims_by_tok(V_WIDTH),
                   tok(FOURIER_WIDTH), tok(FOURIER_WIDTH)),
        out_shape=out_shapes,
        compiler_params=pltpu.CompilerParams(
            dimension_semantics=("arbitrary",), vmem_limit_bytes=48 * MIB),
        name="inproj",
    )(x, gmix, w_qkv_t, w_f, gq, gk, cos_t, sin_t, dc)


def _attn_kernel(bound_ref, lq1_ref, lk1_ref, lq2_ref, lk2_ref, gsub_ref, qt_ref, kt_ref, vt_ref,
                 o_ref, lam_ref, vaug_ref):
    @pl.when((pl.program_id(0) == 0) & (pl.program_id(1) == 0))
    def _():
        a1 = jnp.sum(lq1_ref[...] * lk1_ref[...], axis=-1, keepdims=True)
        a2 = jnp.sum(lq2_ref[...] * lk2_ref[...], axis=-1, keepdims=True)
        lam_ref[...] = jnp.exp(a1) - jnp.exp(a2) + LAMBDA_INIT

    lam = lam_ref[...]
    k = kt_ref[0].T
    vaug_ref[:V_HEAD_DIM, :] = vt_ref[0]
    ones_row = lax.broadcasted_iota(jnp.int32, (ONES_ROWS, SEQ), 0) == 0
    vaug_ref[V_HEAD_DIM:, :] = jnp.where(ones_row, 1.0, 0.0).astype(BF16)
    v_aug = vaug_ref[...]
    gsub = gsub_ref[...]
    tq = Q_TILE
    first_map = lax.broadcasted_iota(jnp.int32, (2 * HEAD_DIM, tq), 0) < HEAD_DIM

    def block(i, shift_by_max):
        cols = pl.ds(pl.multiple_of(i * tq, tq), tq)
        q_t = qt_ref[0, :, cols]
        zero = jnp.zeros_like(q_t)
        qq_t = jnp.concatenate([jnp.where(first_map, q_t, zero), jnp.where(first_map, zero, q_t)],
                               axis=1)
        s_t = _dot(k, qq_t)
        if shift_by_max:
            s_t = s_t - jnp.max(s_t, axis=0, keepdims=True)
        e = jnp.exp2(s_t).astype(BF16)
        acc = _dot(v_aug, e)
        inv = 1.0 / acc[V_HEAD_DIM:V_HEAD_DIM + 1, :]
        o_t = acc[:V_HEAD_DIM, :tq] * inv[:, :tq] - acc[:V_HEAD_DIM, tq:] * (lam * inv[:, tq:])
        ms = jnp.mean(o_t * o_t, axis=0, keepdims=True)
        o_n = o_t * lax.rsqrt(ms + EPS) * gsub
        o_ref[0, cols, :] = o_n.T.astype(BF16)

    def all_blocks(shift_by_max):
        def body(i, carry):
            block(i, shift_by_max)
            return carry
        lax.fori_loop(0, SEQ // tq, body, 0, unroll=SEQ // tq if not shift_by_max else 1)

    no_shift_needed = bound_ref[0] <= SCORE_BOUND_LIMIT
    pl.when(no_shift_needed)(functools.partial(all_blocks, False))
    pl.when(jnp.logical_not(no_shift_needed))(functools.partial(all_blocks, True))


def _attention(bound, lq1, lk1, lq2, lk2, gsub_t, q_t, k_t, v_t):
    n_seq = q_t.shape[0]
    head = pl.BlockSpec((1, SEQ, LANES), lambda b, h: (b, 0, h))
    head_t = pl.BlockSpec((1, V_HEAD_DIM, SEQ), lambda b, h: (b, h, 0))
    return pl.pallas_call(
        _attn_kernel,
        grid=(n_seq, N_HEADS),
        in_specs=[pl.BlockSpec(memory_space=pltpu.SMEM)] + [_resident((1, HEAD_DIM))] * 4
        + [_resident((V_HEAD_DIM, Q_TILE)), head_t, head_t, head_t],
        out_specs=head,
        out_shape=jax.ShapeDtypeStruct((n_seq, SEQ, V_WIDTH), BF16),
        scratch_shapes=[pltpu.VMEM((1, 1), F32),
                        pltpu.VMEM((V_HEAD_DIM + ONES_ROWS, SEQ), BF16)],
        compiler_params=pltpu.CompilerParams(
            dimension_semantics=("arbitrary", "arbitrary"), vmem_limit_bytes=40 * MIB),
        name="diff_attn",
    )(bound, lq1, lk1, lq2, lk2, gsub_t, q_t, k_t, v_t)


def _seq_dft_kernel(cs_ref, nsn_ref, y1_ref, y2_ref, o_ref):
    y1 = y1_ref[0]
    y2 = y2_ref[0]

    def body(i, carry):
        rows = pl.ds(pl.multiple_of(i * DFT_ROWS, DFT_ROWS), DFT_ROWS)
        acc = _dot(cs_ref[rows, :], y1) + _dot(nsn_ref[rows, :], y2)
        o_ref[0, rows, :] = acc.astype(BF16)
        return carry

    lax.fori_loop(0, SEQ // DFT_ROWS, body, 0)


def _seq_dft(cs, nsn, y1, y2):
    n_seq = y1.shape[0]
    seq = pl.BlockSpec((1, SEQ, FOURIER_WIDTH), lambda b: (b, 0, 0))
    return pl.pallas_call(
        _seq_dft_kernel,
        grid=(n_seq,),
        in_specs=[_resident((SEQ, SEQ)), _resident((SEQ, SEQ)), seq, seq],
        out_specs=seq,
        out_shape=jax.ShapeDtypeStruct((n_seq, SEQ, FOURIER_WIDTH), BF16),
        compiler_params=pltpu.CompilerParams(
            dimension_semantics=("arbitrary",), vmem_limit_bytes=48 * MIB),
        name="seq_dft",
    )(cs, nsn, y1, y2)


def _merge_mlp_kernel(x_ref, att_ref, four_ref, gmix_ref, wgate_ref, bgate_ref, wattn_ref,
                      wfour_ref, wout_ref, gmlp_ref, wup_ref, wdown_ref, y_ref):
    x = x_ref[...]
    h = _rmsnorm_rows(x, gmix_ref[...]).astype(BF16)
    gates = jax.nn.sigmoid(_dot(h, wgate_ref[...]) + bgate_ref[...])
    attn_out = _dot(att_ref[...], wattn_ref[...])
    four_out = _dot(four_ref[...], wfour_ref[...])
    mixed = gates[:, :D_MODEL] * attn_out + gates[:, D_MODEL:] * four_out
    x1 = x + _dot(mixed.astype(BF16), wout_ref[...])
    h2 = _rmsnorm_rows(x1, gmlp_ref[...]).astype(BF16)
    u = jnp.square(jnp.maximum(_dot(h2, wup_ref[...]), 0.0)).astype(BF16)
    y_ref[...] = x1 + _dot(u, wdown_ref[...])


def _merge_mlp(x, att, four, gmix, wgate, bgate, wattn, wfour, wout, gmlp, wup, wdown):
    n_tok = x.shape[0]
    tm = OUT_TILE
    tok = lambda w: pl.BlockSpec((tm, w), lambda i: (i, 0))
    return pl.pallas_call(
        _merge_mlp_kernel,
        grid=(n_tok // tm,),
        in_specs=[tok(D_MODEL), tok(V_WIDTH), tok(FOURIER_WIDTH), _resident((1, D_MODEL)),
                  _resident((D_MODEL, 2 * D_MODEL)), _resident((1, 2 * D_MODEL)),
                  _resident((V_WIDTH, D_MODEL)), _resident((FOURIER_WIDTH, D_MODEL)),
                  _resident((D_MODEL, D_MODEL)), _resident((1, D_MODEL)),
                  _resident((D_MODEL, D_FF)), _resident((D_FF, D_MODEL))],
        out_specs=tok(D_MODEL),
        out_shape=jax.ShapeDtypeStruct((n_tok, D_MODEL), F32),
        compiler_params=pltpu.CompilerParams(
            dimension_semantics=("arbitrary",), vmem_limit_bytes=56 * MIB),
        name="merge_mlp",
    )(x, att, four, gmix, wgate, bgate, wattn, wfour, wout, gmlp, wup, wdown)


def _rope_tables():
    half = HEAD_DIM // 2
    freqs = ROPE_THETA ** (-jnp.arange(half, dtype=F32) * 2.0 / HEAD_DIM)
    ang = jnp.arange(SEQ, dtype=F32)[:, None] * freqs[None, :]
    return jnp.cos(ang).T, jnp.sin(ang).T


def _dft_angles(n):
    idx = jnp.arange(n, dtype=jnp.int32)
    kn = (idx[:, None] * idx[None, :]) % n
    return kn.astype(F32) * (2.0 * math.pi / n)


def _seq_dft_tables(norm):
    k = jnp.arange(SEQ, dtype=jnp.int32)[:, None]
    step = 2.0 * math.pi / SEQ
    ang_a = ((k * (jnp.arange(SEQ // LANES, dtype=jnp.int32) * LANES)[None, :]) % SEQ).astype(F32) * step
    ang_b = ((k * jnp.arange(LANES, dtype=jnp.int32)[None, :]) % SEQ).astype(F32) * step
    ca, sa = jnp.cos(ang_a)[:, :, None] * norm, jnp.sin(ang_a)[:, :, None] * norm
    cb, sb = jnp.cos(ang_b)[:, None, :], jnp.sin(ang_b)[:, None, :]
    cs = (ca * cb - sa * sb).reshape(SEQ, SEQ)
    nsn = (-(sa * cb + ca * sb)).reshape(SEQ, SEQ)
    return cs.astype(BF16), nsn.astype(BF16)


def _dft_tables():
    norm = 1.0 / math.sqrt(SEQ * FOURIER_GROUP)
    cs, nsn = _seq_dft_tables(norm)
    ang_c = _dft_angles(FOURIER_GROUP)
    dc = jnp.concatenate([jnp.cos(ang_c), jnp.sin(ang_c)], axis=1).astype(BF16)
    return cs, nsn, dc


def kernel(x_prompt, x_sample, g_mix, w_in, g_q, g_k, lam_q1, lam_k1, lam_q2, lam_k2, g_sub,
           w_attn_br, w_four_br, w_gate, b_gate, w_out, g_mlp, w_up, w_down):
    row = lambda p: p[0].reshape(1, -1).astype(F32)
    as_bf16 = lambda w: w[0].astype(BF16)
    cos_t, sin_t = _rope_tables()
    cs, nsn, dc = _dft_tables()
    col = lambda p: jnp.broadcast_to(p[0].astype(F32)[:, None], (p.shape[1], IN_TILE))
    gq = col(g_q) * (HEAD_DIM ** -0.5 * math.log2(math.e))
    gk = col(g_k)
    score_bound = (HEAD_DIM * jnp.max(jnp.abs(gq)) * jnp.max(jnp.abs(gk))).reshape(1)
    gsub_t = jnp.broadcast_to(g_sub[0].astype(F32)[:, None] * (1.0 - LAMBDA_INIT),
                              (V_HEAD_DIM, Q_TILE))
    gmix = row(g_mix)
    lams = (row(lam_q1), row(lam_k1), row(lam_q2), row(lam_k2))
    w_in_b, w_gate_b, w_attn_b, w_four_b, w_out_b, w_up_b, w_down_b = map(
        as_bf16, (w_in, w_gate, w_attn_br, w_four_br, w_out, w_up, w_down))
    w_qkv_t = w_in_b[:, :2 * QK_WIDTH + V_WIDTH].T
    w_f = w_in_b[:, 2 * QK_WIDTH + V_WIDTH:]

    def layer(x3):
        n_seq = x3.shape[0]
        x = x3.reshape(n_seq * SEQ, D_MODEL)
        q_t, k_t, v_t, y1, y2 = _inproj(x, gmix, w_qkv_t, w_f, gq, gk, cos_t, sin_t, dc)
        seq3 = lambda a: a.reshape(n_seq, SEQ, a.shape[-1])
        att = _attention(score_bound, *lams, gsub_t, q_t, k_t, v_t)
        four = _seq_dft(cs, nsn, seq3(y1), seq3(y2))
        y = _merge_mlp(x, att.reshape(n_seq * SEQ, V_WIDTH),
                       four.reshape(n_seq * SEQ, FOURIER_WIDTH), gmix, w_gate_b, row(b_gate),
                       w_attn_b, w_four_b, w_out_b, row(g_mlp), w_up_b, w_down_b)
        return y.reshape(n_seq, SEQ, D_MODEL)

    return layer(x_prompt), layer(x_sample)
```

```python
import functools
import math

import jax
import jax.numpy as jnp
from jax import lax
from jax.experimental import pallas as pl
from jax.experimental.pallas import tpu as pltpu

D_MODEL = 1024
SEQ = 2048
N_HEADS = 8
HEAD_DIM = 64
V_HEAD_DIM = 2 * HEAD_DIM
QK_WIDTH = 2 * N_HEADS * HEAD_DIM
V_WIDTH = N_HEADS * V_HEAD_DIM
N_FOURIER_GROUPS = 4
FOURIER_GROUP = 128
FOURIER_WIDTH = N_FOURIER_GROUPS * FOURIER_GROUP
IN_WIDTH = 2 * QK_WIDTH + V_WIDTH + FOURIER_WIDTH
D_FF = 4 * D_MODEL
ROPE_THETA = 10000.0
EPS = 1e-6
LAMBDA_INIT = 0.8 - 0.6 * math.exp(-0.3 * 0)

LANES = 128
IN_TILE = 512
IN_ROWS = 512
OUT_TILE = 512
Q_TILE = 256
REV_TILE = 256
ONES_ROWS = 16
SCORE_BOUND_LIMIT = 48.0
MIB = 1024 * 1024

BF16 = jnp.bfloat16
F32 = jnp.float32


def _dot(a, b):
    return jnp.dot(a, b, preferred_element_type=F32)


def _rmsnorm_rows(x, g):
    ms = jnp.mean(x * x, axis=-1, keepdims=True)
    return x * lax.rsqrt(ms + EPS) * g


def _resident(shape):
    return pl.BlockSpec(shape, lambda *_: (0,) * len(shape), pipeline_mode=pl.Buffered(1))


def _map_norm_rope_t(z, gain, cos, sin):
    ss = jnp.sum(z * z, axis=0, keepdims=True)
    zn = z * lax.rsqrt(ss * (1.0 / HEAD_DIM) + EPS) * gain
    half = HEAD_DIM // 2
    x1, x2 = zn[:half], zn[half:]
    return x1 * cos - x2 * sin, x2 * cos + x1 * sin


def _inproj_kernel(x_ref, gmix_ref, wt_ref, wf_ref, gq_ref, gk_ref, cos_ref, sin_ref, dc_ref,
                   qt_ref, kt_ref, vt_ref, y1_ref, y2_ref):
    h = _rmsnorm_rows(x_ref[...], gmix_ref[...]).astype(BF16)
    h_t = h.T
    cos = cos_ref[...]
    sin = sin_ref[...]
    half = HEAD_DIM // 2

    for off, g_ref, o_ref in ((0, gq_ref, qt_ref), (QK_WIDTH, gk_ref, kt_ref)):
        gain = g_ref[...]
        for c in range(QK_WIDTH // IN_ROWS):
            z_t = _dot(wt_ref[off + c * IN_ROWS:off + (c + 1) * IN_ROWS, :], h_t)
            for m in range(IN_ROWS // HEAD_DIM):
                lo, hi = _map_norm_rope_t(z_t[m * HEAD_DIM:(m + 1) * HEAD_DIM], gain, cos, sin)
                r0 = c * IN_ROWS + m * HEAD_DIM
                o_ref[0, r0:r0 + half, :] = lo.astype(BF16)
                o_ref[0, r0 + half:r0 + HEAD_DIM, :] = hi.astype(BF16)

    for c in range(V_WIDTH // IN_ROWS):
        rows = slice(c * IN_ROWS, (c + 1) * IN_ROWS)
        vt_ref[0, rows, :] = _dot(wt_ref[2 * QK_WIDTH + c * IN_ROWS:2 * QK_WIDTH + (c + 1) * IN_ROWS, :],
                                  h_t).astype(BF16)

    f = _dot(h, wf_ref[...]).astype(BF16)
    dc = dc_ref[...]
    for g in range(N_FOURIER_GROUPS):
        sl = slice(g * FOURIER_GROUP, (g + 1) * FOURIER_GROUP)
        y = _dot(f[:, sl], dc)
        y1_ref[:, sl] = y[:, :FOURIER_GROUP].astype(BF16)
        y2_ref[:, sl] = y[:, FOURIER_GROUP:].astype(BF16)


def _inproj(x, gmix, w_qkv_t, w_f, gq, gk, cos_t, sin_t, dc):
    n_tok = x.shape[0]
    n_seq = n_tok // SEQ
    tm = IN_TILE
    pos_blocks = SEQ // tm
    tok = lambda w: pl.BlockSpec((tm, w), lambda i: (i, 0))
    rope = pl.BlockSpec((HEAD_DIM // 2, tm), lambda i: (0, i % pos_blocks))
    dims_by_tok = lambda w: pl.BlockSpec((1, w, tm), lambda i: (i // pos_blocks, 0, i % pos_blocks))
    out_shapes = (
        jax.ShapeDtypeStruct((n_seq, QK_WIDTH, SEQ), BF16),
        jax.ShapeDtypeStruct((n_seq, QK_WIDTH, SEQ), BF16),
        jax.ShapeDtypeStruct((n_seq, V_WIDTH, SEQ), BF16),
        jax.ShapeDtypeStruct((n_tok, FOURIER_WIDTH), BF16),
        jax.ShapeDtypeStruct((n_tok, FOURIER_WIDTH), BF16),
    )
    return pl.pallas_call(
        _inproj_kernel,
        grid=(n_tok // tm,),
        in_specs=[tok(D_MODEL), _resident((1, D_MODEL)),
                  _resident((2 * QK_WIDTH + V_WIDTH, D_MODEL)), _resident((D_MODEL, FOURIER_WIDTH)),
                  _resident((HEAD_DIM, tm)), _resident((HEAD_DIM, tm)), rope, rope,
                  _resident((FOURIER_GROUP, 2 * FOURIER_GROUP))],
        out_specs=(dims_by_tok(QK_WIDTH), dims_by_tok(QK_WIDTH), dims_by_tok(V_WIDTH),
                   tok(FOURIER_WIDTH), tok(FOURIER_WIDTH)),
        out_shape=out_shapes,
        compiler_params=pltpu.CompilerParams(
            dimension_semantics=("arbitrary",), vmem_limit_bytes=48 * MIB),
        name="inproj",
    )(x, gmix, w_qkv_t, w_f, gq, gk, cos_t, sin_t, dc)


def _attn_kernel(bound_ref, lq1_ref, lk1_ref, lq2_ref, lk2_ref, gsub_ref, qt_ref, kt_ref, vt_ref,
                 o_ref, lam_ref, vaug_ref):
    @pl.when((pl.program_id(0) == 0) & (pl.program_id(1) == 0))
    def _():
        a1 = jnp.sum(lq1_ref[...] * lk1_ref[...], axis=-1, keepdims=True)
        a2 = jnp.sum(lq2_ref[...] * lk2_ref[...], axis=-1, keepdims=True)
        lam_ref[...] = jnp.exp(a1) - jnp.exp(a2) + LAMBDA_INIT

    lam = lam_ref[...]
    k = kt_ref[0].T
    vaug_ref[:V_HEAD_DIM, :] = vt_ref[0]
    ones_row = lax.broadcasted_iota(jnp.int32, (ONES_ROWS, SEQ), 0) == 0
    vaug_ref[V_HEAD_DIM:, :] = jnp.where(ones_row, 1.0, 0.0).astype(BF16)
    v_aug = vaug_ref[...]
    gsub = gsub_ref[...]
    tq = Q_TILE
    first_map = lax.broadcasted_iota(jnp.int32, (2 * HEAD_DIM, tq), 0) < HEAD_DIM

    def block(i, shift_by_max):
        cols = pl.ds(pl.multiple_of(i * tq, tq), tq)
        q_t = qt_ref[0, :, cols]
        zero = jnp.zeros_like(q_t)
        qq_t = jnp.concatenate([jnp.where(first_map, q_t, zero), jnp.where(first_map, zero, q_t)],
                               axis=1)
        s_t = _dot(k, qq_t)
        if shift_by_max:
            s_t = s_t - jnp.max(s_t, axis=0, keepdims=True)
        e = jnp.exp2(s_t).astype(BF16)
        acc = _dot(v_aug, e)
        inv = 1.0 / acc[V_HEAD_DIM:V_HEAD_DIM + 1, :]
        o_t = acc[:V_HEAD_DIM, :tq] * inv[:, :tq] - acc[:V_HEAD_DIM, tq:] * (lam * inv[:, tq:])
        ms = jnp.mean(o_t * o_t, axis=0, keepdims=True)
        o_n = o_t * lax.rsqrt(ms + EPS) * gsub
        o_ref[0, cols, :] = o_n.T.astype(BF16)

    def all_blocks(shift_by_max):
        def body(i, carry):
            block(i, shift_by_max)
            return carry
        lax.fori_loop(0, SEQ // tq, body, 0, unroll=SEQ // tq if not shift_by_max else 1)

    no_shift_needed = bound_ref[0] <= SCORE_BOUND_LIMIT
    pl.when(no_shift_needed)(functools.partial(all_blocks, False))
    pl.when(jnp.logical_not(no_shift_needed))(functools.partial(all_blocks, True))


def _attention(bound, lq1, lk1, lq2, lk2, gsub_t, q_t, k_t, v_t):
    n_seq = q_t.shape[0]
    head = pl.BlockSpec((1, SEQ, LANES), lambda b, h: (b, 0, h))
    head_t = pl.BlockSpec((1, V_HEAD_DIM, SEQ), lambda b, h: (b, h, 0))
    return pl.pallas_call(
        _attn_kernel,
        grid=(n_seq, N_HEADS),
        in_specs=[pl.BlockSpec(memory_space=pltpu.SMEM)] + [_resident((1, HEAD_DIM))] * 4
        + [_resident((V_HEAD_DIM, Q_TILE)), head_t, head_t, head_t],
        out_specs=head,
        out_shape=jax.ShapeDtypeStruct((n_seq, SEQ, V_WIDTH), BF16),
        scratch_shapes=[pltpu.VMEM((1, 1), F32),
                        pltpu.VMEM((V_HEAD_DIM + ONES_ROWS, SEQ), BF16)],
        compiler_params=pltpu.CompilerParams(
            dimension_semantics=("arbitrary", "arbitrary"), vmem_limit_bytes=40 * MIB),
        name="diff_attn",
    )(bound, lq1, lk1, lq2, lk2, gsub_t, q_t, k_t, v_t)


def _seq_dft_kernel(cq_ref, sq_ref, jrev_ref, y1_ref, y2_ref, o_ref):
    half = SEQ // 2
    norm = 1.0 / math.sqrt(SEQ * FOURIER_GROUP)
    jrev = jrev_ref[...]
    n_tiles = half // REV_TILE
    row = lax.broadcasted_iota(jnp.int32, (half, 1), 0)

    def reversed_shifted(hi):
        parts = [_dot(jrev, hi[(n_tiles - 1 - t) * REV_TILE:(n_tiles - t) * REV_TILE])
                 for t in range(n_tiles)]
        return pltpu.roll(jnp.concatenate(parts, axis=0), 1, 0)

    not_first = row != 0
    r1 = jnp.where(not_first, reversed_shifted(y1_ref[0, half:, :]), 0.0)
    r2 = jnp.where(not_first, reversed_shifted(y2_ref[0, half:, :]), 0.0)
    p = y1_ref[0, :half, :].astype(F32) + r1
    q = y2_ref[0, :half, :].astype(F32) - r2
    a = _dot(cq_ref[...], p.astype(BF16))
    b = _dot(sq_ref[...], q.astype(BF16))
    alt = jnp.where(row % 2 == 0, 1.0, -1.0)
    y_mid = y1_ref[0, half:half + 1, :].astype(F32) * norm
    corr = alt * y_mid
    o_ref[0, :half, :] = (a - b + corr).astype(BF16)
    upper = reversed_shifted((a + b + corr).astype(BF16))
    mid = norm * jnp.sum(alt * p, axis=0, keepdims=True) + y_mid
    o_ref[0, half:, :] = jnp.where(not_first, upper, mid).astype(BF16)


def _seq_dft(cq, sq, jrev, y1, y2):
    n_seq = y1.shape[0]
    seq = pl.BlockSpec((1, SEQ, FOURIER_WIDTH), lambda b: (b, 0, 0))
    return pl.pallas_call(
        _seq_dft_kernel,
        grid=(n_seq,),
        in_specs=[_resident((SEQ // 2, SEQ // 2)), _resident((SEQ // 2, SEQ // 2)),
                  _resident((REV_TILE, REV_TILE)), seq, seq],
        out_specs=seq,
        out_shape=jax.ShapeDtypeStruct((n_seq, SEQ, FOURIER_WIDTH), BF16),
        compiler_params=pltpu.CompilerParams(
            dimension_semantics=("arbitrary",), vmem_limit_bytes=48 * MIB),
        name="seq_dft",
    )(cq, sq, jrev, y1, y2)


def _merge_mlp_kernel(x_ref, att_ref, four_ref, gmix_ref, wgate_ref, bgate_ref, wattn_ref,
                      wfour_ref, wout_ref, gmlp_ref, wup_ref, wdown_ref, y_ref):
    x = x_ref[...]
    h = _rmsnorm_rows(x, gmix_ref[...]).astype(BF16)
    gates = jax.nn.sigmoid(_dot(h, wgate_ref[...]) + bgate_ref[...])
    attn_out = _dot(att_ref[...], wattn_ref[...])
    four_out = _dot(four_ref[...], wfour_ref[...])
    mixed = gates[:, :D_MODEL] * attn_out + gates[:, D_MODEL:] * four_out
    x1 = x + _dot(mixed.astype(BF16), wout_ref[...])
    h2 = _rmsnorm_rows(x1, gmlp_ref[...]).astype(BF16)
    u = jnp.square(jnp.maximum(_dot(h2, wup_ref[...]), 0.0)).astype(BF16)
    y_ref[...] = x1 + _dot(u, wdown_ref[...])


def _merge_mlp(x, att, four, gmix, wgate, bgate, wattn, wfour, wout, gmlp, wup, wdown):
    n_tok = x.shape[0]
    tm = OUT_TILE
    tok = lambda w: pl.BlockSpec((tm, w), lambda i: (i, 0))
    return pl.pallas_call(
        _merge_mlp_kernel,
        grid=(n_tok // tm,),
        in_specs=[tok(D_MODEL), tok(V_WIDTH), tok(FOURIER_WIDTH), _resident((1, D_MODEL)),
                  _resident((D_MODEL, 2 * D_MODEL)), _resident((1, 2 * D_MODEL)),
                  _resident((V_WIDTH, D_MODEL)), _resident((FOURIER_WIDTH, D_MODEL)),
                  _resident((D_MODEL, D_MODEL)), _resident((1, D_MODEL)),
                  _resident((D_MODEL, D_FF)), _resident((D_FF, D_MODEL))],
        out_specs=tok(D_MODEL),
        out_shape=jax.ShapeDtypeStruct((n_tok, D_MODEL), F32),
        compiler_params=pltpu.CompilerParams(
            dimension_semantics=("arbitrary",), vmem_limit_bytes=56 * MIB),
        name="merge_mlp",
    )(x, att, four, gmix, wgate, bgate, wattn, wfour, wout, gmlp, wup, wdown)


def _rope_tables():
    half = HEAD_DIM // 2
    freqs = ROPE_THETA ** (-jnp.arange(half, dtype=F32) * 2.0 / HEAD_DIM)
    ang = jnp.arange(SEQ, dtype=F32)[:, None] * freqs[None, :]
    return jnp.cos(ang).T, jnp.sin(ang).T


def _dft_angles(n):
    idx = jnp.arange(n, dtype=jnp.int32)
    kn = (idx[:, None] * idx[None, :]) % n
    return kn.astype(F32) * (2.0 * math.pi / n)


def _seq_dft_tables(norm):
    half = SEQ // 2
    k = jnp.arange(half, dtype=jnp.int32)[:, None]
    step = 2.0 * math.pi / SEQ
    ang_a = ((k * (jnp.arange(half // LANES, dtype=jnp.int32) * LANES)[None, :]) % SEQ).astype(F32) * step
    ang_b = ((k * jnp.arange(LANES, dtype=jnp.int32)[None, :]) % SEQ).astype(F32) * step
    ca, sa = jnp.cos(ang_a)[:, :, None] * norm, jnp.sin(ang_a)[:, :, None] * norm
    cb, sb = jnp.cos(ang_b)[:, None, :], jnp.sin(ang_b)[:, None, :]
    cq = (ca * cb - sa * sb).reshape(half, half)
    sq = (sa * cb + ca * sb).reshape(half, half)
    return cq.astype(BF16), sq.astype(BF16)


def _dft_tables():
    norm = 1.0 / math.sqrt(SEQ * FOURIER_GROUP)
    cq, sq = _seq_dft_tables(norm)
    jrev = jnp.flip(jnp.eye(REV_TILE, dtype=BF16), axis=0)
    ang_c = _dft_angles(FOURIER_GROUP)
    dc = jnp.concatenate([jnp.cos(ang_c), jnp.sin(ang_c)], axis=1).astype(BF16)
    return cq, sq, jrev, dc


def kernel(x_prompt, x_sample, g_mix, w_in, g_q, g_k, lam_q1, lam_k1, lam_q2, lam_k2, g_sub,
           w_attn_br, w_four_br, w_gate, b_gate, w_out, g_mlp, w_up, w_down):
    row = lambda p: p[0].reshape(1, -1).astype(F32)
    as_bf16 = lambda w: w[0].astype(BF16)
    cos_t, sin_t = _rope_tables()
    cq, sq, jrev, dc = _dft_tables()
    col = lambda p: jnp.broadcast_to(p[0].astype(F32)[:, None], (p.shape[1], IN_TILE))
    gq = col(g_q) * (HEAD_DIM ** -0.5 * math.log2(math.e))
    gk = col(g_k)
    score_bound = (HEAD_DIM * jnp.max(jnp.abs(gq)) * jnp.max(jnp.abs(gk))).reshape(1)
    gsub_t = jnp.broadcast_to(g_sub[0].astype(F32)[:, None] * (1.0 - LAMBDA_INIT),
                              (V_HEAD_DIM, Q_TILE))
    gmix = row(g_mix)
    lams = (row(lam_q1), row(lam_k1), row(lam_q2), row(lam_k2))
    w_in_b, w_gate_b, w_attn_b, w_four_b, w_out_b, w_up_b, w_down_b = map(
        as_bf16, (w_in, w_gate, w_attn_br, w_four_br, w_out, w_up, w_down))
    w_qkv_t = w_in_b[:, :2 * QK_WIDTH + V_WIDTH].T
    w_f = w_in_b[:, 2 * QK_WIDTH + V_WIDTH:]

    def layer(x3):
        n_seq = x3.shape[0]
        x = x3.reshape(n_seq * SEQ, D_MODEL)
        q_t, k_t, v_t, y1, y2 = _inproj(x, gmix, w_qkv_t, w_f, gq, gk, cos_t, sin_t, dc)
        seq3 = lambda a: a.reshape(n_seq, SEQ, a.shape[-1])
        att = _attention(score_bound, *lams, gsub_t, q_t, k_t, v_t)
        four = _seq_dft(cq, sq, jrev, seq3(y1), seq3(y2))
        y = _merge_mlp(x, att.reshape(n_seq * SEQ, V_WIDTH),
                       four.reshape(n_seq * SEQ, FOURIER_WIDTH), gmix, w_gate_b, row(b_gate),
                       w_attn_b, w_four_b, w_out_b, row(g_mlp), w_up_b, w_down_b)
        return y.reshape(n_seq, SEQ, D_MODEL)

    return layer(x_prompt), layer(x_sample)
```

```python
import functools
import math

import jax
import jax.numpy as jnp
from jax import lax
from jax.experimental import pallas as pl
from jax.experimental.pallas import tpu as pltpu

D_MODEL = 1024
SEQ = 2048
N_HEADS = 8
HEAD_DIM = 64
V_HEAD_DIM = 2 * HEAD_DIM
QK_WIDTH = 2 * N_HEADS * HEAD_DIM
V_WIDTH = N_HEADS * V_HEAD_DIM
N_FOURIER_GROUPS = 4
FOURIER_GROUP = 128
FOURIER_WIDTH = N_FOURIER_GROUPS * FOURIER_GROUP
IN_WIDTH = 2 * QK_WIDTH + V_WIDTH + FOURIER_WIDTH
D_FF = 4 * D_MODEL
ROPE_THETA = 10000.0
EPS = 1e-6
LAMBDA_INIT = 0.8 - 0.6 * math.exp(-0.3 * 0)

LANES = 128
IN_TILE = 512
IN_ROWS = 512
OUT_TILE = 512
Q_TILE = 256
REV_TILE = 256
SCORE_BOUND_LIMIT = 48.0
MIB = 1024 * 1024

BF16 = jnp.bfloat16
F32 = jnp.float32


def _dot(a, b):
    return jnp.dot(a, b, preferred_element_type=F32)


def _rmsnorm_rows(x, g):
    ms = jnp.mean(x * x, axis=-1, keepdims=True)
    return x * lax.rsqrt(ms + EPS) * g


def _resident(shape):
    return pl.BlockSpec(shape, lambda *_: (0,) * len(shape), pipeline_mode=pl.Buffered(1))


def _map_norm_rope_t(z, gain, cos, sin):
    ss = jnp.sum(z * z, axis=0, keepdims=True)
    zn = z * lax.rsqrt(ss * (1.0 / HEAD_DIM) + EPS) * gain
    half = HEAD_DIM // 2
    x1, x2 = zn[:half], zn[half:]
    return x1 * cos - x2 * sin, x2 * cos + x1 * sin


def _inproj_kernel(x_ref, gmix_ref, wt_ref, wf_ref, gq_ref, gk_ref, cos_ref, sin_ref, dc_ref,
                   qt_ref, k_ref, vt_ref, y1_ref, y2_ref):
    h = _rmsnorm_rows(x_ref[...], gmix_ref[...]).astype(BF16)

    f = _dot(h, wf_ref[...]).astype(BF16)
    dc = dc_ref[...]
    for g in range(N_FOURIER_GROUPS):
        sl = slice(g * FOURIER_GROUP, (g + 1) * FOURIER_GROUP)
        y = _dot(f[:, sl], dc)
        y1_ref[:, sl] = y[:, :FOURIER_GROUP].astype(BF16)
        y2_ref[:, sl] = y[:, FOURIER_GROUP:].astype(BF16)

    h_t = h.T
    cos = cos_ref[...]
    sin = sin_ref[...]
    half = HEAD_DIM // 2

    def normed_chunk(off, c, gain):
        z_t = _dot(wt_ref[off + c * IN_ROWS:off + (c + 1) * IN_ROWS, :], h_t)
        return [_map_norm_rope_t(z_t[m * HEAD_DIM:(m + 1) * HEAD_DIM], gain, cos, sin)
                for m in range(IN_ROWS // HEAD_DIM)]

    gq = gq_ref[...]
    for c in range(QK_WIDTH // IN_ROWS):
        for m, (lo, hi) in enumerate(normed_chunk(0, c, gq)):
            r0 = c * IN_ROWS + m * HEAD_DIM
            qt_ref[0, r0:r0 + half, :] = lo.astype(BF16)
            qt_ref[0, r0 + half:r0 + HEAD_DIM, :] = hi.astype(BF16)

    gk = gk_ref[...]
    for c in range(QK_WIDTH // IN_ROWS):
        pieces = [p for lo_hi in normed_chunk(QK_WIDTH, c, gk) for p in lo_hi]
        k_ref[:, c * IN_ROWS:(c + 1) * IN_ROWS] = jnp.concatenate(pieces, axis=0).T.astype(BF16)

    for c in range(V_WIDTH // IN_ROWS):
        rows = slice(c * IN_ROWS, (c + 1) * IN_ROWS)
        vt_ref[0, rows, :] = _dot(wt_ref[2 * QK_WIDTH + c * IN_ROWS:2 * QK_WIDTH + (c + 1) * IN_ROWS, :],
                                  h_t).astype(BF16)


def _inproj(x, gmix, w_qkv_t, w_f, gq, gk, cos_t, sin_t, dc):
    n_tok = x.shape[0]
    n_seq = n_tok // SEQ
    tm = IN_TILE
    pos_blocks = SEQ // tm
    tok = lambda w: pl.BlockSpec((tm, w), lambda i: (i, 0))
    rope = pl.BlockSpec((HEAD_DIM // 2, tm), lambda i: (0, i % pos_blocks))
    dims_by_tok = lambda w: pl.BlockSpec((1, w, tm), lambda i: (i // pos_blocks, 0, i % pos_blocks))
    out_shapes = (
        jax.ShapeDtypeStruct((n_seq, QK_WIDTH, SEQ), BF16),
        jax.ShapeDtypeStruct((n_tok, QK_WIDTH), BF16),
        jax.ShapeDtypeStruct((n_seq, V_WIDTH, SEQ), BF16),
        jax.ShapeDtypeStruct((n_tok, FOURIER_WIDTH), BF16),
        jax.ShapeDtypeStruct((n_tok, FOURIER_WIDTH), BF16),
    )
    return pl.pallas_call(
        _inproj_kernel,
        grid=(n_tok // tm,),
        in_specs=[tok(D_MODEL), _resident((1, D_MODEL)),
                  _resident((2 * QK_WIDTH + V_WIDTH, D_MODEL)), _resident((D_MODEL, FOURIER_WIDTH)),
                  _resident((HEAD_DIM, tm)), _resident((HEAD_DIM, tm)), rope, rope,
                  _resident((FOURIER_GROUP, 2 * FOURIER_GROUP))],
        out_specs=(dims_by_tok(QK_WIDTH), tok(QK_WIDTH), dims_by_tok(V_WIDTH),
                   tok(FOURIER_WIDTH), tok(FOURIER_WIDTH)),
        out_shape=out_shapes,
        compiler_params=pltpu.CompilerParams(
            dimension_semantics=("arbitrary",), vmem_limit_bytes=48 * MIB),
        name="inproj",
    )(x, gmix, w_qkv_t, w_f, gq, gk, cos_t, sin_t, dc)


def _attn_kernel(bound_ref, lq1_ref, lk1_ref, lq2_ref, lk2_ref, gsub_ref, qt_ref, k_ref, vt_ref,
                 o_ref, lam_ref):
    @pl.when((pl.program_id(0) == 0) & (pl.program_id(1) == 0))
    def _():
        a1 = jnp.sum(lq1_ref[...] * lk1_ref[...], axis=-1, keepdims=True)
        a2 = jnp.sum(lq2_ref[...] * lk2_ref[...], axis=-1, keepdims=True)
        lam_ref[...] = jnp.exp(a1) - jnp.exp(a2) + LAMBDA_INIT

    lam = lam_ref[...]
    k = k_ref[0]
    v_t = vt_ref[0]
    gsub = gsub_ref[...]
    tq = Q_TILE
    first_map = lax.broadcasted_iota(jnp.int32, (2 * HEAD_DIM, tq), 0) < HEAD_DIM

    def block(i, shift_by_max):
        cols = pl.ds(pl.multiple_of(i * tq, tq), tq)
        q_t = qt_ref[0, :, cols]
        zero = jnp.zeros_like(q_t)
        qq_t = jnp.concatenate([jnp.where(first_map, q_t, zero), jnp.where(first_map, zero, q_t)],
                               axis=1)
        s_t = _dot(k, qq_t)
        if shift_by_max:
            s_t = s_t - jnp.max(s_t, axis=0, keepdims=True)
        e = jnp.exp2(s_t)
        inv = 1.0 / jnp.sum(e, axis=0, keepdims=True)
        acc = _dot(v_t, e.astype(BF16))
        o_t = acc[:, :tq] * inv[:, :tq] - acc[:, tq:] * (lam * inv[:, tq:])
        ms = jnp.mean(o_t * o_t, axis=0, keepdims=True)
        o_n = o_t * lax.rsqrt(ms + EPS) * gsub
        o_ref[0, cols, :] = o_n.T.astype(BF16)

    def all_blocks(shift_by_max):
        def body(i, carry):
            block(i, shift_by_max)
            return carry
        lax.fori_loop(0, SEQ // tq, body, 0, unroll=SEQ // tq if not shift_by_max else 1)

    no_shift_needed = bound_ref[0] <= SCORE_BOUND_LIMIT
    pl.when(no_shift_needed)(functools.partial(all_blocks, False))
    pl.when(jnp.logical_not(no_shift_needed))(functools.partial(all_blocks, True))


def _attention(bound, lq1, lk1, lq2, lk2, gsub_t, q_t, k, v_t):
    n_seq = q_t.shape[0]
    head = pl.BlockSpec((1, SEQ, LANES), lambda b, h: (b, 0, h))
    head_t = pl.BlockSpec((1, V_HEAD_DIM, SEQ), lambda b, h: (b, h, 0))
    return pl.pallas_call(
        _attn_kernel,
        grid=(n_seq, N_HEADS),
        in_specs=[pl.BlockSpec(memory_space=pltpu.SMEM)] + [_resident((1, HEAD_DIM))] * 4
        + [_resident((V_HEAD_DIM, Q_TILE)), head_t, head, head_t],
        out_specs=head,
        out_shape=jax.ShapeDtypeStruct((n_seq, SEQ, V_WIDTH), BF16),
        scratch_shapes=[pltpu.VMEM((1, 1), F32)],
        compiler_params=pltpu.CompilerParams(
            dimension_semantics=("arbitrary", "arbitrary"), vmem_limit_bytes=40 * MIB),
        name="diff_attn",
    )(bound, lq1, lk1, lq2, lk2, gsub_t, q_t, k, v_t)


def _seq_dft_kernel(cq_ref, sq_ref, jrev_ref, y1_ref, y2_ref, o_ref):
    half = SEQ // 2
    norm = 1.0 / math.sqrt(SEQ * FOURIER_GROUP)
    jrev = jrev_ref[...]
    n_tiles = half // REV_TILE
    row = lax.broadcasted_iota(jnp.int32, (half, 1), 0)

    def reversed_shifted(hi):
        parts = [_dot(jrev, hi[(n_tiles - 1 - t) * REV_TILE:(n_tiles - t) * REV_TILE])
                 for t in range(n_tiles)]
        return pltpu.roll(jnp.concatenate(parts, axis=0), 1, 0)

    not_first = row != 0
    r1 = jnp.where(not_first, reversed_shifted(y1_ref[0, half:, :]), 0.0)
    r2 = jnp.where(not_first, reversed_shifted(y2_ref[0, half:, :]), 0.0)
    p = y1_ref[0, :half, :].astype(F32) + r1
    q = y2_ref[0, :half, :].astype(F32) - r2
    a = _dot(cq_ref[...], p.astype(BF16))
    b = _dot(sq_ref[...], q.astype(BF16))
    alt = jnp.where(row % 2 == 0, 1.0, -1.0)
    y_mid = y1_ref[0, half:half + 1, :].astype(F32) * norm
    corr = alt * y_mid
    o_ref[0, :half, :] = (a - b + corr).astype(BF16)
    upper = reversed_shifted((a + b + corr).astype(BF16))
    mid = norm * jnp.sum(alt * p, axis=0, keepdims=True) + y_mid
    o_ref[0, half:, :] = jnp.where(not_first, upper, mid).astype(BF16)


def _seq_dft(cq, sq, jrev, y1, y2):
    n_seq = y1.shape[0]
    seq = pl.BlockSpec((1, SEQ, FOURIER_WIDTH), lambda b: (b, 0, 0))
    return pl.pallas_call(
        _seq_dft_kernel,
        grid=(n_seq,),
        in_specs=[_resident((SEQ // 2, SEQ // 2)), _resident((SEQ // 2, SEQ // 2)),
                  _resident((REV_TILE, REV_TILE)), seq, seq],
        out_specs=seq,
        out_shape=jax.ShapeDtypeStruct((n_seq, SEQ, FOURIER_WIDTH), BF16),
        compiler_params=pltpu.CompilerParams(
            dimension_semantics=("arbitrary",), vmem_limit_bytes=48 * MIB),
        name="seq_dft",
    )(cq, sq, jrev, y1, y2)


def _merge_mlp_kernel(x_ref, att_ref, four_ref, gmix_ref, wgate_ref, bgate_ref, wattn_ref,
                      wfour_ref, wout_ref, gmlp_ref, wup_ref, wdown_ref, y_ref):
    x = x_ref[...]
    h = _rmsnorm_rows(x, gmix_ref[...]).astype(BF16)
    gates = jax.nn.sigmoid(_dot(h, wgate_ref[...]) + bgate_ref[...])
    attn_out = _dot(att_ref[...], wattn_ref[...])
    four_out = _dot(four_ref[...], wfour_ref[...])
    mixed = gates[:, :D_MODEL] * attn_out + gates[:, D_MODEL:] * four_out
    x1 = x + _dot(mixed.astype(BF16), wout_ref[...])
    h2 = _rmsnorm_rows(x1, gmlp_ref[...]).astype(BF16)
    u = jnp.square(jnp.maximum(_dot(h2, wup_ref[...]), 0.0)).astype(BF16)
    y_ref[...] = x1 + _dot(u, wdown_ref[...])


def _merge_mlp(x, att, four, gmix, wgate, bgate, wattn, wfour, wout, gmlp, wup, wdown):
    n_tok = x.shape[0]
    tm = OUT_TILE
    tok = lambda w: pl.BlockSpec((tm, w), lambda i: (i, 0))
    return pl.pallas_call(
        _merge_mlp_kernel,
        grid=(n_tok // tm,),
        in_specs=[tok(D_MODEL), tok(V_WIDTH), tok(FOURIER_WIDTH), _resident((1, D_MODEL)),
                  _resident((D_MODEL, 2 * D_MODEL)), _resident((1, 2 * D_MODEL)),
                  _resident((V_WIDTH, D_MODEL)), _resident((FOURIER_WIDTH, D_MODEL)),
                  _resident((D_MODEL, D_MODEL)), _resident((1, D_MODEL)),
                  _resident((D_MODEL, D_FF)), _resident((D_FF, D_MODEL))],
        out_specs=tok(D_MODEL),
        out_shape=jax.ShapeDtypeStruct((n_tok, D_MODEL), F32),
        compiler_params=pltpu.CompilerParams(
            dimension_semantics=("arbitrary",), vmem_limit_bytes=56 * MIB),
        name="merge_mlp",
    )(x, att, four, gmix, wgate, bgate, wattn, wfour, wout, gmlp, wup, wdown)


def _rope_tables():
    half = HEAD_DIM // 2
    freqs = ROPE_THETA ** (-jnp.arange(half, dtype=F32) * 2.0 / HEAD_DIM)
    ang = jnp.arange(SEQ, dtype=F32)[:, None] * freqs[None, :]
    return jnp.cos(ang).T, jnp.sin(ang).T


def _dft_angles(n):
    idx = jnp.arange(n, dtype=jnp.int32)
    kn = (idx[:, None] * idx[None, :]) % n
    return kn.astype(F32) * (2.0 * math.pi / n)


def _seq_dft_tables(norm):
    half = SEQ // 2
    k = jnp.arange(half, dtype=jnp.int32)[:, None]
    step = 2.0 * math.pi / SEQ
    ang_a = ((k * (jnp.arange(half // LANES, dtype=jnp.int32) * LANES)[None, :]) % SEQ).astype(F32) * step
    ang_b = ((k * jnp.arange(LANES, dtype=jnp.int32)[None, :]) % SEQ).astype(F32) * step
    ca, sa = jnp.cos(ang_a)[:, :, None] * norm, jnp.sin(ang_a)[:, :, None] * norm
    cb, sb = jnp.cos(ang_b)[:, None, :], jnp.sin(ang_b)[:, None, :]
    cq = (ca * cb - sa * sb).reshape(half, half)
    sq = (sa * cb + ca * sb).reshape(half, half)
    return cq.astype(BF16), sq.astype(BF16)


def _dft_tables():
    norm = 1.0 / math.sqrt(SEQ * FOURIER_GROUP)
    cq, sq = _seq_dft_tables(norm)
    jrev = jnp.flip(jnp.eye(REV_TILE, dtype=BF16), axis=0)
    ang_c = _dft_angles(FOURIER_GROUP)
    dc = jnp.concatenate([jnp.cos(ang_c), jnp.sin(ang_c)], axis=1).astype(BF16)
    return cq, sq, jrev, dc


def kernel(x_prompt, x_sample, g_mix, w_in, g_q, g_k, lam_q1, lam_k1, lam_q2, lam_k2, g_sub,
           w_attn_br, w_four_br, w_gate, b_gate, w_out, g_mlp, w_up, w_down):
    row = lambda p: p[0].reshape(1, -1).astype(F32)
    as_bf16 = lambda w: w[0].astype(BF16)
    cos_t, sin_t = _rope_tables()
    cq, sq, jrev, dc = _dft_tables()
    col = lambda p: jnp.broadcast_to(p[0].astype(F32)[:, None], (p.shape[1], IN_TILE))
    gq = col(g_q) * (HEAD_DIM ** -0.5 * math.log2(math.e))
    gk = col(g_k)
    score_bound = (HEAD_DIM * jnp.max(jnp.abs(gq)) * jnp.max(jnp.abs(gk))).reshape(1)
    gsub_t = jnp.broadcast_to(g_sub[0].astype(F32)[:, None] * (1.0 - LAMBDA_INIT),
                              (V_HEAD_DIM, Q_TILE))
    gmix = row(g_mix)
    lams = (row(lam_q1), row(lam_k1), row(lam_q2), row(lam_k2))
    w_in_b, w_gate_b, w_attn_b, w_four_b, w_out_b, w_up_b, w_down_b = map(
        as_bf16, (w_in, w_gate, w_attn_br, w_four_br, w_out, w_up, w_down))
    w_qkv_t = w_in_b[:, :2 * QK_WIDTH + V_WIDTH].T
    w_f = w_in_b[:, 2 * QK_WIDTH + V_WIDTH:]

    def layer(x3):
        n_seq = x3.shape[0]
        x = x3.reshape(n_seq * SEQ, D_MODEL)
        q_t, k, v_t, y1, y2 = _inproj(x, gmix, w_qkv_t, w_f, gq, gk, cos_t, sin_t, dc)
        seq3 = lambda a: a.reshape(n_seq, SEQ, a.shape[-1])
        att = _attention(score_bound, *lams, gsub_t, q_t, seq3(k), v_t)
        four = _seq_dft(cq, sq, jrev, seq3(y1), seq3(y2))
        y = _merge_mlp(x, att.reshape(n_seq * SEQ, V_WIDTH),
                       four.reshape(n_seq * SEQ, FOURIER_WIDTH), gmix, w_gate_b, row(b_gate),
                       w_attn_b, w_four_b, w_out_b, row(g_mlp), w_up_b, w_down_b)
        return y.reshape(n_seq, SEQ, D_MODEL)

    return layer(x_prompt), layer(x_sample)
```

```python
import functools
import math

import jax
import jax.numpy as jnp
from jax import lax
from jax.experimental import pallas as pl
from jax.experimental.pallas import tpu as pltpu

D_MODEL = 1024
SEQ = 2048
N_HEADS = 8
HEAD_DIM = 64
V_HEAD_DIM = 2 * HEAD_DIM
QK_WIDTH = 2 * N_HEADS * HEAD_DIM
V_WIDTH = N_HEADS * V_HEAD_DIM
N_FOURIER_GROUPS = 4
FOURIER_GROUP = 128
FOURIER_WIDTH = N_FOURIER_GROUPS * FOURIER_GROUP
IN_WIDTH = 2 * QK_WIDTH + V_WIDTH + FOURIER_WIDTH
D_FF = 4 * D_MODEL
ROPE_THETA = 10000.0
EPS = 1e-6
LAMBDA_INIT = 0.8 - 0.6 * math.exp(-0.3 * 0)

LANES = 128
IN_TILE = 1024
IN_ROWS = 256
OUT_TILE = 512
Q_TILE = 256
ATTN_HEADS = 2
REV_TILE = 256
SCORE_BOUND_LIMIT = 48.0
MIB = 1024 * 1024

BF16 = jnp.bfloat16
F32 = jnp.float32


def _dot(a, b):
    return jnp.dot(a, b, preferred_element_type=F32)


def _rmsnorm_rows(x, g):
    ms = jnp.mean(x * x, axis=-1, keepdims=True)
    return x * lax.rsqrt(ms + EPS) * g


def _resident(shape):
    return pl.BlockSpec(shape, lambda *_: (0,) * len(shape), pipeline_mode=pl.Buffered(1))


def _map_norm_rope_t(z, gain, cos, sin):
    ss = jnp.sum(z * z, axis=0, keepdims=True)
    zn = z * lax.rsqrt(ss * (1.0 / HEAD_DIM) + EPS) * gain
    half = HEAD_DIM // 2
    x1, x2 = zn[:half], zn[half:]
    return x1 * cos - x2 * sin, x2 * cos + x1 * sin


def _inproj_kernel(x_ref, gmix_ref, wt_ref, wf_ref, gq_ref, gk_ref, cos_ref, sin_ref, dc_ref,
                   qt_ref, k_ref, vt_ref, y1_ref, y2_ref):
    h = _rmsnorm_rows(x_ref[...], gmix_ref[...]).astype(BF16)

    f = _dot(h, wf_ref[...]).astype(BF16)
    dc = dc_ref[...]
    for g in range(N_FOURIER_GROUPS):
        sl = slice(g * FOURIER_GROUP, (g + 1) * FOURIER_GROUP)
        y = _dot(f[:, sl], dc)
        y1_ref[:, sl] = y[:, :FOURIER_GROUP].astype(BF16)
        y2_ref[:, sl] = y[:, FOURIER_GROUP:].astype(BF16)

    h_t = h.T
    cos = cos_ref[...]
    sin = sin_ref[...]
    half = HEAD_DIM // 2

    def normed_chunk(off, c, gain):
        z_t = _dot(wt_ref[off + c * IN_ROWS:off + (c + 1) * IN_ROWS, :], h_t)
        return [_map_norm_rope_t(z_t[m * HEAD_DIM:(m + 1) * HEAD_DIM], gain, cos, sin)
                for m in range(IN_ROWS // HEAD_DIM)]

    gq = gq_ref[...]
    for c in range(QK_WIDTH // IN_ROWS):
        for m, (lo, hi) in enumerate(normed_chunk(0, c, gq)):
            r0 = c * IN_ROWS + m * HEAD_DIM
            qt_ref[0, r0:r0 + half, :] = lo.astype(BF16)
            qt_ref[0, r0 + half:r0 + HEAD_DIM, :] = hi.astype(BF16)

    gk = gk_ref[...]
    for c in range(QK_WIDTH // IN_ROWS):
        pieces = [p for lo_hi in normed_chunk(QK_WIDTH, c, gk) for p in lo_hi]
        k_ref[:, c * IN_ROWS:(c + 1) * IN_ROWS] = jnp.concatenate(pieces, axis=0).T.astype(BF16)

    for c in range(V_WIDTH // IN_ROWS):
        rows = slice(c * IN_ROWS, (c + 1) * IN_ROWS)
        vt_ref[0, rows, :] = _dot(wt_ref[2 * QK_WIDTH + c * IN_ROWS:2 * QK_WIDTH + (c + 1) * IN_ROWS, :],
                                  h_t).astype(BF16)


def _inproj(x, gmix, w_qkv_t, w_f, gq, gk, cos_t, sin_t, dc):
    n_tok = x.shape[0]
    n_seq = n_tok // SEQ
    tm = IN_TILE
    pos_blocks = SEQ // tm
    tok = lambda w: pl.BlockSpec((tm, w), lambda i: (i, 0))
    rope = pl.BlockSpec((HEAD_DIM // 2, tm), lambda i: (0, i % pos_blocks))
    dims_by_tok = lambda w: pl.BlockSpec((1, w, tm), lambda i: (i // pos_blocks, 0, i % pos_blocks))
    out_shapes = (
        jax.ShapeDtypeStruct((n_seq, QK_WIDTH, SEQ), BF16),
        jax.ShapeDtypeStruct((n_tok, QK_WIDTH), BF16),
        jax.ShapeDtypeStruct((n_seq, V_WIDTH, SEQ), BF16),
        jax.ShapeDtypeStruct((n_tok, FOURIER_WIDTH), BF16),
        jax.ShapeDtypeStruct((n_tok, FOURIER_WIDTH), BF16),
    )
    return pl.pallas_call(
        _inproj_kernel,
        grid=(n_tok // tm,),
        in_specs=[tok(D_MODEL), _resident((1, D_MODEL)),
                  _resident((2 * QK_WIDTH + V_WIDTH, D_MODEL)), _resident((D_MODEL, FOURIER_WIDTH)),
                  _resident((HEAD_DIM, tm)), _resident((HEAD_DIM, tm)), rope, rope,
                  _resident((FOURIER_GROUP, 2 * FOURIER_GROUP))],
        out_specs=(dims_by_tok(QK_WIDTH), tok(QK_WIDTH), dims_by_tok(V_WIDTH),
                   tok(FOURIER_WIDTH), tok(FOURIER_WIDTH)),
        out_shape=out_shapes,
        compiler_params=pltpu.CompilerParams(
            dimension_semantics=("arbitrary",), vmem_limit_bytes=48 * MIB),
        name="inproj",
    )(x, gmix, w_qkv_t, w_f, gq, gk, cos_t, sin_t, dc)


def _attn_kernel(bound_ref, lq1_ref, lk1_ref, lq2_ref, lk2_ref, gsub_ref, qt_ref, k_ref, vt_ref,
                 o_ref, lam_ref):
    @pl.when((pl.program_id(0) == 0) & (pl.program_id(1) == 0))
    def _():
        a1 = jnp.sum(lq1_ref[...] * lk1_ref[...], axis=-1, keepdims=True)
        a2 = jnp.sum(lq2_ref[...] * lk2_ref[...], axis=-1, keepdims=True)
        lam_ref[...] = jnp.exp(a1) - jnp.exp(a2) + LAMBDA_INIT

    lam = lam_ref[...]
    gsub = gsub_ref[...]
    tq = Q_TILE
    first_map = lax.broadcasted_iota(jnp.int32, (2 * HEAD_DIM, tq), 0) < HEAD_DIM

    def block(hh, i, shift_by_max):
        head = slice(hh * V_HEAD_DIM, (hh + 1) * V_HEAD_DIM)
        k = k_ref[0, :, head]
        v_t = vt_ref[0, head, :]
        cols = pl.ds(pl.multiple_of(i * tq, tq), tq)
        q_t = qt_ref[0, head, cols]
        zero = jnp.zeros_like(q_t)
        qq_t = jnp.concatenate([jnp.where(first_map, q_t, zero), jnp.where(first_map, zero, q_t)],
                               axis=1)
        s_t = _dot(k, qq_t)
        if shift_by_max:
            s_t = s_t - jnp.max(s_t, axis=0, keepdims=True)
        e = jnp.exp2(s_t)
        inv = 1.0 / jnp.sum(e, axis=0, keepdims=True)
        acc = _dot(v_t, e.astype(BF16))
        o_t = acc[:, :tq] * inv[:, :tq] - acc[:, tq:] * (lam * inv[:, tq:])
        ms = jnp.mean(o_t * o_t, axis=0, keepdims=True)
        o_n = o_t * lax.rsqrt(ms + EPS) * gsub
        o_ref[0, cols, head] = o_n.T.astype(BF16)

    def all_blocks(shift_by_max):
        for hh in range(ATTN_HEADS):
            def body(i, carry):
                block(hh, i, shift_by_max)
                return carry
            lax.fori_loop(0, SEQ // tq, body, 0, unroll=SEQ // tq if not shift_by_max else 1)

    no_shift_needed = bound_ref[0] <= SCORE_BOUND_LIMIT
    pl.when(no_shift_needed)(functools.partial(all_blocks, False))
    pl.when(jnp.logical_not(no_shift_needed))(functools.partial(all_blocks, True))


def _attention(bound, lq1, lk1, lq2, lk2, gsub_t, q_t, k, v_t):
    n_seq = q_t.shape[0]
    width = ATTN_HEADS * V_HEAD_DIM
    head = pl.BlockSpec((1, SEQ, width), lambda b, h: (b, 0, h))
    head_t = pl.BlockSpec((1, width, SEQ), lambda b, h: (b, h, 0))
    return pl.pallas_call(
        _attn_kernel,
        grid=(n_seq, N_HEADS // ATTN_HEADS),
        in_specs=[pl.BlockSpec(memory_space=pltpu.SMEM)] + [_resident((1, HEAD_DIM))] * 4
        + [_resident((V_HEAD_DIM, Q_TILE)), head_t, head, head_t],
        out_specs=head,
        out_shape=jax.ShapeDtypeStruct((n_seq, SEQ, V_WIDTH), BF16),
        scratch_shapes=[pltpu.VMEM((1, 1), F32)],
        compiler_params=pltpu.CompilerParams(
            dimension_semantics=("arbitrary", "arbitrary"), vmem_limit_bytes=40 * MIB),
        name="diff_attn",
    )(bound, lq1, lk1, lq2, lk2, gsub_t, q_t, k, v_t)


def _seq_dft_kernel(cq_ref, sq_ref, jrev_ref, y1_ref, y2_ref, o_ref):
    half = SEQ // 2
    norm = 1.0 / math.sqrt(SEQ * FOURIER_GROUP)
    jrev = jrev_ref[...]
    n_tiles = half // REV_TILE
    row = lax.broadcasted_iota(jnp.int32, (half, 1), 0)

    def reversed_shifted(hi):
        parts = [_dot(jrev, hi[(n_tiles - 1 - t) * REV_TILE:(n_tiles - t) * REV_TILE])
                 for t in range(n_tiles)]
        return pltpu.roll(jnp.concatenate(parts, axis=0), 1, 0)

    not_first = row != 0
    r1 = jnp.where(not_first, reversed_shifted(y1_ref[0, half:, :]), 0.0)
    r2 = jnp.where(not_first, reversed_shifted(y2_ref[0, half:, :]), 0.0)
    p = y1_ref[0, :half, :].astype(F32) + r1
    q = y2_ref[0, :half, :].astype(F32) - r2
    a = _dot(cq_ref[...], p.astype(BF16))
    b = _dot(sq_ref[...], q.astype(BF16))
    alt = jnp.where(row % 2 == 0, 1.0, -1.0)
    y_mid = y1_ref[0, half:half + 1, :].astype(F32) * norm
    corr = alt * y_mid
    o_ref[0, :half, :] = (a - b + corr).astype(BF16)
    upper = reversed_shifted((a + b + corr).astype(BF16))
    mid = norm * jnp.sum(alt * p, axis=0, keepdims=True) + y_mid
    o_ref[0, half:, :] = jnp.where(not_first, upper, mid).astype(BF16)


def _seq_dft(cq, sq, jrev, y1, y2):
    n_seq = y1.shape[0]
    seq = pl.BlockSpec((1, SEQ, FOURIER_WIDTH), lambda b: (b, 0, 0))
    return pl.pallas_call(
        _seq_dft_kernel,
        grid=(n_seq,),
        in_specs=[_resident((SEQ // 2, SEQ // 2)), _resident((SEQ // 2, SEQ // 2)),
                  _resident((REV_TILE, REV_TILE)), seq, seq],
        out_specs=seq,
        out_shape=jax.ShapeDtypeStruct((n_seq, SEQ, FOURIER_WIDTH), BF16),
        compiler_params=pltpu.CompilerParams(
            dimension_semantics=("arbitrary",), vmem_limit_bytes=48 * MIB),
        name="seq_dft",
    )(cq, sq, jrev, y1, y2)


def _merge_mlp_kernel(x_ref, att_ref, four_ref, gmix_ref, wgate_ref, bgate_ref, wattn_ref,
                      wfour_ref, wout_ref, gmlp_ref, wup_ref, wdown_ref, y_ref):
    x = x_ref[...]
    h = _rmsnorm_rows(x, gmix_ref[...]).astype(BF16)
    gates = jax.nn.sigmoid(_dot(h, wgate_ref[...]) + bgate_ref[...])
    attn_out = _dot(att_ref[...], wattn_ref[...])
    four_out = _dot(four_ref[...], wfour_ref[...])
    mixed = gates[:, :D_MODEL] * attn_out + gates[:, D_MODEL:] * four_out
    x1 = x + _dot(mixed.astype(BF16), wout_ref[...])
    h2 = _rmsnorm_rows(x1, gmlp_ref[...]).astype(BF16)
    u = jnp.square(jnp.maximum(_dot(h2, wup_ref[...]), 0.0)).astype(BF16)
    y_ref[...] = x1 + _dot(u, wdown_ref[...])


def _merge_mlp(x, att, four, gmix, wgate, bgate, wattn, wfour, wout, gmlp, wup, wdown):
    n_tok = x.shape[0]
    tm = OUT_TILE
    tok = lambda w: pl.BlockSpec((tm, w), lambda i: (i, 0))
    return pl.pallas_call(
        _merge_mlp_kernel,
        grid=(n_tok // tm,),
        in_specs=[tok(D_MODEL), tok(V_WIDTH), tok(FOURIER_WIDTH), _resident((1, D_MODEL)),
                  _resident((D_MODEL, 2 * D_MODEL)), _resident((1, 2 * D_MODEL)),
                  _resident((V_WIDTH, D_MODEL)), _resident((FOURIER_WIDTH, D_MODEL)),
                  _resident((D_MODEL, D_MODEL)), _resident((1, D_MODEL)),
                  _resident((D_MODEL, D_FF)), _resident((D_FF, D_MODEL))],
        out_specs=tok(D_MODEL),
        out_shape=jax.ShapeDtypeStruct((n_tok, D_MODEL), F32),
        compiler_params=pltpu.CompilerParams(
            dimension_semantics=("arbitrary",), vmem_limit_bytes=56 * MIB),
        name="merge_mlp",
    )(x, att, four, gmix, wgate, bgate, wattn, wfour, wout, gmlp, wup, wdown)


def _rope_tables():
    half = HEAD_DIM // 2
    freqs = ROPE_THETA ** (-jnp.arange(half, dtype=F32) * 2.0 / HEAD_DIM)
    ang = jnp.arange(SEQ, dtype=F32)[:, None] * freqs[None, :]
    return jnp.cos(ang).T, jnp.sin(ang).T


def _dft_angles(n):
    idx = jnp.arange(n, dtype=jnp.int32)
    kn = (idx[:, None] * idx[None, :]) % n
    return kn.astype(F32) * (2.0 * math.pi / n)


def _seq_dft_tables(norm):
    half = SEQ // 2
    k = jnp.arange(half, dtype=jnp.int32)[:, None]
    step = 2.0 * math.pi / SEQ
    ang_a = ((k * (jnp.arange(half // LANES, dtype=jnp.int32) * LANES)[None, :]) % SEQ).astype(F32) * step
    ang_b = ((k * jnp.arange(LANES, dtype=jnp.int32)[None, :]) % SEQ).astype(F32) * step
    ca, sa = jnp.cos(ang_a)[:, :, None] * norm, jnp.sin(ang_a)[:, :, None] * norm
    cb, sb = jnp.cos(ang_b)[:, None, :], jnp.sin(ang_b)[:, None, :]
    cq = (ca * cb - sa * sb).reshape(half, half)
    sq = (sa * cb + ca * sb).reshape(half, half)
    return cq.astype(BF16), sq.astype(BF16)


def _dft_tables():
    norm = 1.0 / math.sqrt(SEQ * FOURIER_GROUP)
    cq, sq = _seq_dft_tables(norm)
    jrev = jnp.flip(jnp.eye(REV_TILE, dtype=BF16), axis=0)
    ang_c = _dft_angles(FOURIER_GROUP)
    dc = jnp.concatenate([jnp.cos(ang_c), jnp.sin(ang_c)], axis=1).astype(BF16)
    return cq, sq, jrev, dc


def kernel(x_prompt, x_sample, g_mix, w_in, g_q, g_k, lam_q1, lam_k1, lam_q2, lam_k2, g_sub,
           w_attn_br, w_four_br, w_gate, b_gate, w_out, g_mlp, w_up, w_down):
    row = lambda p: p[0].reshape(1, -1).astype(F32)
    as_bf16 = lambda w: w[0].astype(BF16)
    cos_t, sin_t = _rope_tables()
    cq, sq, jrev, dc = _dft_tables()
    col = lambda p: jnp.broadcast_to(p[0].astype(F32)[:, None], (p.shape[1], IN_TILE))
    gq = col(g_q) * (HEAD_DIM ** -0.5 * math.log2(math.e))
    gk = col(g_k)
    score_bound = (HEAD_DIM * jnp.max(jnp.abs(gq)) * jnp.max(jnp.abs(gk))).reshape(1)
    gsub_t = jnp.broadcast_to(g_sub[0].astype(F32)[:, None] * (1.0 - LAMBDA_INIT),
                              (V_HEAD_DIM, Q_TILE))
    gmix = row(g_mix)
    lams = (row(lam_q1), row(lam_k1), row(lam_q2), row(lam_k2))
    w_in_b, w_gate_b, w_attn_b, w_four_b, w_out_b, w_up_b, w_down_b = map(
        as_bf16, (w_in, w_gate, w_attn_br, w_four_br, w_out, w_up, w_down))
    w_qkv_t = w_in_b[:, :2 * QK_WIDTH + V_WIDTH].T
    w_f = w_in_b[:, 2 * QK_WIDTH + V_WIDTH:]

    def layer(x3):
        n_seq = x3.shape[0]
        x = x3.reshape(n_seq * SEQ, D_MODEL)
        q_t, k, v_t, y1, y2 = _inproj(x, gmix, w_qkv_t, w_f, gq, gk, cos_t, sin_t, dc)
        seq3 = lambda a: a.reshape(n_seq, SEQ, a.shape[-1])
        att = _attention(score_bound, *lams, gsub_t, q_t, seq3(k), v_t)
        four = _seq_dft(cq, sq, jrev, seq3(y1), seq3(y2))
        y = _merge_mlp(x, att.reshape(n_seq * SEQ, V_WIDTH),
                       four.reshape(n_seq * SEQ, FOURIER_WIDTH), gmix, w_gate_b, row(b_gate),
                       w_attn_b, w_four_b, w_out_b, row(g_mlp), w_up_b, w_down_b)
        return y.reshape(n_seq, SEQ, D_MODEL)

    return layer(x_prompt), layer(x_sample)
```

```python
import functools
import math

import jax
import jax.numpy as jnp
from jax import lax
from jax.experimental import pallas as pl
from jax.experimental.pallas import tpu as pltpu

D_MODEL = 1024
SEQ = 2048
N_HEADS = 8
HEAD_DIM = 64
V_HEAD_DIM = 2 * HEAD_DIM
QK_WIDTH = 2 * N_HEADS * HEAD_DIM
V_WIDTH = N_HEADS * V_HEAD_DIM
N_FOURIER_GROUPS = 4
FOURIER_GROUP = 128
FOURIER_WIDTH = N_FOURIER_GROUPS * FOURIER_GROUP
IN_WIDTH = 2 * QK_WIDTH + V_WIDTH + FOURIER_WIDTH
D_FF = 4 * D_MODEL
ROPE_THETA = 10000.0
EPS = 1e-6
LAMBDA_INIT = 0.8 - 0.6 * math.exp(-0.3 * 0)

LANES = 128
IN_TILE = 1024
IN_ROWS = 256
OUT_TILE = 512
Q_TILE = 256
ATTN_HEADS = 4
REV_TILE = 256
SCORE_BOUND_LIMIT = 48.0
MIB = 1024 * 1024

BF16 = jnp.bfloat16
F32 = jnp.float32


def _dot(a, b):
    return jnp.dot(a, b, preferred_element_type=F32)


def _rmsnorm_rows(x, g):
    ms = jnp.mean(x * x, axis=-1, keepdims=True)
    return x * lax.rsqrt(ms + EPS) * g


def _resident(shape):
    return pl.BlockSpec(shape, lambda *_: (0,) * len(shape), pipeline_mode=pl.Buffered(1))


def _map_norm_rope_t(z, gain, cos, sin):
    ss = jnp.sum(z * z, axis=0, keepdims=True)
    zn = z * lax.rsqrt(ss * (1.0 / HEAD_DIM) + EPS) * gain
    half = HEAD_DIM // 2
    x1, x2 = zn[:half], zn[half:]
    return x1 * cos - x2 * sin, x2 * cos + x1 * sin


def _inproj_kernel(x_ref, gmix_ref, wt_ref, wf_ref, gq_ref, gk_ref, cos_ref, sin_ref, dc_ref,
                   qt_ref, k_ref, vt_ref, y1_ref, y2_ref):
    h = _rmsnorm_rows(x_ref[...], gmix_ref[...]).astype(BF16)

    f = _dot(h, wf_ref[...]).astype(BF16)
    dc = dc_ref[...]
    for g in range(N_FOURIER_GROUPS):
        sl = slice(g * FOURIER_GROUP, (g + 1) * FOURIER_GROUP)
        y = _dot(f[:, sl], dc)
        y1_ref[:, sl] = y[:, :FOURIER_GROUP].astype(BF16)
        y2_ref[:, sl] = y[:, FOURIER_GROUP:].astype(BF16)

    h_t = h.T
    cos = cos_ref[...]
    sin = sin_ref[...]
    half = HEAD_DIM // 2

    def normed_chunk(off, c, gain):
        z_t = _dot(wt_ref[off + c * IN_ROWS:off + (c + 1) * IN_ROWS, :], h_t)
        return [_map_norm_rope_t(z_t[m * HEAD_DIM:(m + 1) * HEAD_DIM], gain, cos, sin)
                for m in range(IN_ROWS // HEAD_DIM)]

    gq = gq_ref[...]
    for c in range(QK_WIDTH // IN_ROWS):
        for m, (lo, hi) in enumerate(normed_chunk(0, c, gq)):
            r0 = c * IN_ROWS + m * HEAD_DIM
            qt_ref[0, r0:r0 + half, :] = lo.astype(BF16)
            qt_ref[0, r0 + half:r0 + HEAD_DIM, :] = hi.astype(BF16)

    gk = gk_ref[...]
    for c in range(QK_WIDTH // IN_ROWS):
        pieces = [p for lo_hi in normed_chunk(QK_WIDTH, c, gk) for p in lo_hi]
        k_ref[:, c * IN_ROWS:(c + 1) * IN_ROWS] = jnp.concatenate(pieces, axis=0).T.astype(BF16)

    for c in range(V_WIDTH // IN_ROWS):
        rows = slice(c * IN_ROWS, (c + 1) * IN_ROWS)
        vt_ref[0, rows, :] = _dot(wt_ref[2 * QK_WIDTH + c * IN_ROWS:2 * QK_WIDTH + (c + 1) * IN_ROWS, :],
                                  h_t).astype(BF16)


def _inproj(x, gmix, w_qkv_t, w_f, gq, gk, cos_t, sin_t, dc):
    n_tok = x.shape[0]
    n_seq = n_tok // SEQ
    tm = IN_TILE
    pos_blocks = SEQ // tm
    tok = lambda w: pl.BlockSpec((tm, w), lambda i: (i, 0))
    rope = pl.BlockSpec((HEAD_DIM // 2, tm), lambda i: (0, i % pos_blocks))
    dims_by_tok = lambda w: pl.BlockSpec((1, w, tm), lambda i: (i // pos_blocks, 0, i % pos_blocks))
    out_shapes = (
        jax.ShapeDtypeStruct((n_seq, QK_WIDTH, SEQ), BF16),
        jax.ShapeDtypeStruct((n_tok, QK_WIDTH), BF16),
        jax.ShapeDtypeStruct((n_seq, V_WIDTH, SEQ), BF16),
        jax.ShapeDtypeStruct((n_tok, FOURIER_WIDTH), BF16),
        jax.ShapeDtypeStruct((n_tok, FOURIER_WIDTH), BF16),
    )
    return pl.pallas_call(
        _inproj_kernel,
        grid=(n_tok // tm,),
        in_specs=[tok(D_MODEL), _resident((1, D_MODEL)),
                  _resident((2 * QK_WIDTH + V_WIDTH, D_MODEL)), _resident((D_MODEL, FOURIER_WIDTH)),
                  _resident((HEAD_DIM, tm)), _resident((HEAD_DIM, tm)), rope, rope,
                  _resident((FOURIER_GROUP, 2 * FOURIER_GROUP))],
        out_specs=(dims_by_tok(QK_WIDTH), tok(QK_WIDTH), dims_by_tok(V_WIDTH),
                   tok(FOURIER_WIDTH), tok(FOURIER_WIDTH)),
        out_shape=out_shapes,
        compiler_params=pltpu.CompilerParams(
            dimension_semantics=("arbitrary",), vmem_limit_bytes=48 * MIB),
        name="inproj",
    )(x, gmix, w_qkv_t, w_f, gq, gk, cos_t, sin_t, dc)


def _attn_kernel(bound_ref, lq1_ref, lk1_ref, lq2_ref, lk2_ref, gsub_ref, qt_ref, k_ref, vt_ref,
                 o_ref, lam_ref):
    @pl.when((pl.program_id(0) == 0) & (pl.program_id(1) == 0))
    def _():
        a1 = jnp.sum(lq1_ref[...] * lk1_ref[...], axis=-1, keepdims=True)
        a2 = jnp.sum(lq2_ref[...] * lk2_ref[...], axis=-1, keepdims=True)
        lam_ref[...] = jnp.exp(a1) - jnp.exp(a2) + LAMBDA_INIT

    lam = lam_ref[...]
    gsub = gsub_ref[...]
    tq = Q_TILE
    first_map = lax.broadcasted_iota(jnp.int32, (2 * HEAD_DIM, tq), 0) < HEAD_DIM

    def block(hh, i, shift_by_max):
        head = slice(hh * V_HEAD_DIM, (hh + 1) * V_HEAD_DIM)
        k = k_ref[0, :, head]
        v_t = vt_ref[0, head, :]
        cols = pl.ds(pl.multiple_of(i * tq, tq), tq)
        q_t = qt_ref[0, head, cols]
        zero = jnp.zeros_like(q_t)
        qq_t = jnp.concatenate([jnp.where(first_map, q_t, zero), jnp.where(first_map, zero, q_t)],
                               axis=1)
        s_t = _dot(k, qq_t)
        if shift_by_max:
            s_t = s_t - jnp.max(s_t, axis=0, keepdims=True)
        e = jnp.exp2(s_t)
        inv = 1.0 / jnp.sum(e, axis=0, keepdims=True)
        acc = _dot(v_t, e.astype(BF16))
        o_t = acc[:, :tq] * inv[:, :tq] - acc[:, tq:] * (lam * inv[:, tq:])
        ms = jnp.mean(o_t * o_t, axis=0, keepdims=True)
        o_n = o_t * lax.rsqrt(ms + EPS) * gsub
        o_ref[0, cols, head] = o_n.T.astype(BF16)

    def all_blocks(shift_by_max):
        for hh in range(ATTN_HEADS):
            def body(i, carry):
                block(hh, i, shift_by_max)
                return carry
            lax.fori_loop(0, SEQ // tq, body, 0, unroll=SEQ // tq if not shift_by_max else 1)

    no_shift_needed = bound_ref[0] <= SCORE_BOUND_LIMIT
    pl.when(no_shift_needed)(functools.partial(all_blocks, False))
    pl.when(jnp.logical_not(no_shift_needed))(functools.partial(all_blocks, True))


def _attention(bound, lq1, lk1, lq2, lk2, gsub_t, q_t, k, v_t):
    n_seq = q_t.shape[0]
    width = ATTN_HEADS * V_HEAD_DIM
    head = pl.BlockSpec((1, SEQ, width), lambda b, h: (b, 0, h))
    head_t = pl.BlockSpec((1, width, SEQ), lambda b, h: (b, h, 0))
    return pl.pallas_call(
        _attn_kernel,
        grid=(n_seq, N_HEADS // ATTN_HEADS),
        in_specs=[pl.BlockSpec(memory_space=pltpu.SMEM)] + [_resident((1, HEAD_DIM))] * 4
        + [_resident((V_HEAD_DIM, Q_TILE)), head_t, head, head_t],
        out_specs=head,
        out_shape=jax.ShapeDtypeStruct((n_seq, SEQ, V_WIDTH), BF16),
        scratch_shapes=[pltpu.VMEM((1, 1), F32)],
        compiler_params=pltpu.CompilerParams(
            dimension_semantics=("arbitrary", "arbitrary"), vmem_limit_bytes=40 * MIB),
        name="diff_attn",
    )(bound, lq1, lk1, lq2, lk2, gsub_t, q_t, k, v_t)


def _seq_dft_kernel(cq_ref, sq_ref, jrev_ref, y1_ref, y2_ref, o_ref):
    half = SEQ // 2
    norm = 1.0 / math.sqrt(SEQ * FOURIER_GROUP)
    jrev = jrev_ref[...]
    n_tiles = half // REV_TILE
    row = lax.broadcasted_iota(jnp.int32, (half, 1), 0)

    def reversed_shifted(hi):
        parts = [_dot(jrev, hi[(n_tiles - 1 - t) * REV_TILE:(n_tiles - t) * REV_TILE])
                 for t in range(n_tiles)]
        return pltpu.roll(jnp.concatenate(parts, axis=0), 1, 0)

    not_first = row != 0
    r1 = jnp.where(not_first, reversed_shifted(y1_ref[0, half:, :]), 0.0)
    r2 = jnp.where(not_first, reversed_shifted(y2_ref[0, half:, :]), 0.0)
    p = y1_ref[0, :half, :].astype(F32) + r1
    q = y2_ref[0, :half, :].astype(F32) - r2
    a = _dot(cq_ref[...], p.astype(BF16))
    b = _dot(sq_ref[...], q.astype(BF16))
    alt = jnp.where(row % 2 == 0, 1.0, -1.0)
    y_mid = y1_ref[0, half:half + 1, :].astype(F32) * norm
    corr = alt * y_mid
    o_ref[0, :half, :] = (a - b + corr).astype(BF16)
    upper = reversed_shifted((a + b + corr).astype(BF16))
    mid = norm * jnp.sum(alt * p, axis=0, keepdims=True) + y_mid
    o_ref[0, half:, :] = jnp.where(not_first, upper, mid).astype(BF16)


def _seq_dft(cq, sq, jrev, y1, y2):
    n_seq = y1.shape[0]
    seq = pl.BlockSpec((1, SEQ, FOURIER_WIDTH), lambda b: (b, 0, 0))
    return pl.pallas_call(
        _seq_dft_kernel,
        grid=(n_seq,),
        in_specs=[_resident((SEQ // 2, SEQ // 2)), _resident((SEQ // 2, SEQ // 2)),
                  _resident((REV_TILE, REV_TILE)), seq, seq],
        out_specs=seq,
        out_shape=jax.ShapeDtypeStruct((n_seq, SEQ, FOURIER_WIDTH), BF16),
        compiler_params=pltpu.CompilerParams(
            dimension_semantics=("arbitrary",), vmem_limit_bytes=48 * MIB),
        name="seq_dft",
    )(cq, sq, jrev, y1, y2)


def _merge_mlp_kernel(x_ref, att_ref, four_ref, gmix_ref, wgate_ref, bgate_ref, wattn_ref,
                      wfour_ref, wout_ref, gmlp_ref, wup_ref, wdown_ref, y_ref):
    x = x_ref[...]
    h = _rmsnorm_rows(x, gmix_ref[...]).astype(BF16)
    gates = jax.nn.sigmoid(_dot(h, wgate_ref[...]) + bgate_ref[...])
    attn_out = _dot(att_ref[...], wattn_ref[...])
    four_out = _dot(four_ref[...], wfour_ref[...])
    mixed = gates[:, :D_MODEL] * attn_out + gates[:, D_MODEL:] * four_out
    x1 = x + _dot(mixed.astype(BF16), wout_ref[...])
    h2 = _rmsnorm_rows(x1, gmlp_ref[...]).astype(BF16)
    u = jnp.square(jnp.maximum(_dot(h2, wup_ref[...]), 0.0)).astype(BF16)
    y_ref[...] = x1 + _dot(u, wdown_ref[...])


def _merge_mlp(x, att, four, gmix, wgate, bgate, wattn, wfour, wout, gmlp, wup, wdown):
    n_tok = x.shape[0]
    tm = OUT_TILE
    tok = lambda w: pl.BlockSpec((tm, w), lambda i: (i, 0))
    return pl.pallas_call(
        _merge_mlp_kernel,
        grid=(n_tok // tm,),
        in_specs=[tok(D_MODEL), tok(V_WIDTH), tok(FOURIER_WIDTH), _resident((1, D_MODEL)),
                  _resident((D_MODEL, 2 * D_MODEL)), _resident((1, 2 * D_MODEL)),
                  _resident((V_WIDTH, D_MODEL)), _resident((FOURIER_WIDTH, D_MODEL)),
                  _resident((D_MODEL, D_MODEL)), _resident((1, D_MODEL)),
                  _resident((D_MODEL, D_FF)), _resident((D_FF, D_MODEL))],
        out_specs=tok(D_MODEL),
        out_shape=jax.ShapeDtypeStruct((n_tok, D_MODEL), F32),
        compiler_params=pltpu.CompilerParams(
            dimension_semantics=("arbitrary",), vmem_limit_bytes=56 * MIB),
        name="merge_mlp",
    )(x, att, four, gmix, wgate, bgate, wattn, wfour, wout, gmlp, wup, wdown)


def _rope_tables():
    half = HEAD_DIM // 2
    freqs = ROPE_THETA ** (-jnp.arange(half, dtype=F32) * 2.0 / HEAD_DIM)
    ang = jnp.arange(SEQ, dtype=F32)[:, None] * freqs[None, :]
    return jnp.cos(ang).T, jnp.sin(ang).T


def _dft_angles(n):
    idx = jnp.arange(n, dtype=jnp.int32)
    kn = (idx[:, None] * idx[None, :]) % n
    return kn.astype(F32) * (2.0 * math.pi / n)


def _seq_dft_tables(norm):
    half = SEQ // 2
    k = jnp.arange(half, dtype=jnp.int32)[:, None]
    step = 2.0 * math.pi / SEQ
    ang_a = ((k * (jnp.arange(half // LANES, dtype=jnp.int32) * LANES)[None, :]) % SEQ).astype(F32) * step
    ang_b = ((k * jnp.arange(LANES, dtype=jnp.int32)[None, :]) % SEQ).astype(F32) * step
    ca, sa = jnp.cos(ang_a)[:, :, None] * norm, jnp.sin(ang_a)[:, :, None] * norm
    cb, sb = jnp.cos(ang_b)[:, None, :], jnp.sin(ang_b)[:, None, :]
    cq = (ca * cb - sa * sb).reshape(half, half)
    sq = (sa * cb + ca * sb).reshape(half, half)
    return cq.astype(BF16), sq.astype(BF16)


def _dft_tables():
    norm = 1.0 / math.sqrt(SEQ * FOURIER_GROUP)
    cq, sq = _seq_dft_tables(norm)
    jrev = jnp.flip(jnp.eye(REV_TILE, dtype=BF16), axis=0)
    ang_c = _dft_angles(FOURIER_GROUP)
    dc = jnp.concatenate([jnp.cos(ang_c), jnp.sin(ang_c)], axis=1).astype(BF16)
    return cq, sq, jrev, dc


def kernel(x_prompt, x_sample, g_mix, w_in, g_q, g_k, lam_q1, lam_k1, lam_q2, lam_k2, g_sub,
           w_attn_br, w_four_br, w_gate, b_gate, w_out, g_mlp, w_up, w_down):
    row = lambda p: p[0].reshape(1, -1).astype(F32)
    as_bf16 = lambda w: w[0].astype(BF16)
    cos_t, sin_t = _rope_tables()
    cq, sq, jrev, dc = _dft_tables()
    col = lambda p: jnp.broadcast_to(p[0].astype(F32)[:, None], (p.shape[1], IN_TILE))
    gq = col(g_q) * (HEAD_DIM ** -0.5 * math.log2(math.e))
    gk = col(g_k)
    score_bound = (HEAD_DIM * jnp.max(jnp.abs(gq)) * jnp.max(jnp.abs(gk))).reshape(1)
    gsub_t = jnp.broadcast_to(g_sub[0].astype(F32)[:, None] * (1.0 - LAMBDA_INIT),
                              (V_HEAD_DIM, Q_TILE))
    gmix = row(g_mix)
    lams = (row(lam_q1), row(lam_k1), row(lam_q2), row(lam_k2))
    w_gate_b, w_attn_b, w_four_b, w_out_b, w_up_b, w_down_b = map(
        as_bf16, (w_gate, w_attn_br, w_four_br, w_out, w_up, w_down))
    w_qkv_t = w_in[0, :, :2 * QK_WIDTH + V_WIDTH].T.astype(BF16)
    w_f = w_in[0, :, 2 * QK_WIDTH + V_WIDTH:].astype(BF16)

    def layer(x3):
        n_seq = x3.shape[0]
        x = x3.reshape(n_seq * SEQ, D_MODEL)
        q_t, k, v_t, y1, y2 = _inproj(x, gmix, w_qkv_t, w_f, gq, gk, cos_t, sin_t, dc)
        seq3 = lambda a: a.reshape(n_seq, SEQ, a.shape[-1])
        att = _attention(score_bound, *lams, gsub_t, q_t, seq3(k), v_t)
        four = _seq_dft(cq, sq, jrev, seq3(y1), seq3(y2))
        y = _merge_mlp(x, att.reshape(n_seq * SEQ, V_WIDTH),
                       four.reshape(n_seq * SEQ, FOURIER_WIDTH), gmix, w_gate_b, row(b_gate),
                       w_attn_b, w_four_b, w_out_b, row(g_mlp), w_up_b, w_down_b)
        return y.reshape(n_seq, SEQ, D_MODEL)

    return layer(x_prompt), layer(x_sample)
```

```python
import functools
import math

import jax
import jax.numpy as jnp
from jax import lax
from jax.experimental import pallas as pl
from jax.experimental.pallas import tpu as pltpu

D_MODEL = 1024
SEQ = 2048
N_HEADS = 8
HEAD_DIM = 64
V_HEAD_DIM = 2 * HEAD_DIM
QK_WIDTH = 2 * N_HEADS * HEAD_DIM
V_WIDTH = N_HEADS * V_HEAD_DIM
N_FOURIER_GROUPS = 4
FOURIER_GROUP = 128
FOURIER_WIDTH = N_FOURIER_GROUPS * FOURIER_GROUP
IN_WIDTH = 2 * QK_WIDTH + V_WIDTH + FOURIER_WIDTH
D_FF = 4 * D_MODEL
ROPE_THETA = 10000.0
EPS = 1e-6
LAMBDA_INIT = 0.8 - 0.6 * math.exp(-0.3 * 0)

LANES = 128
IN_TILE = 1024
IN_ROWS = 256
IN_COLS = 256
OUT_TILE = 512
Q_TILE = 256
ATTN_HEADS = 2
REV_TILE = 256
SCORE_BOUND_LIMIT = 48.0
MIB = 1024 * 1024
VMEM_INPROJ = 48 * MIB
VMEM_ATTN = 40 * MIB
VMEM_DFT = 48 * MIB
VMEM_MERGE = 56 * MIB

BF16 = jnp.bfloat16
F32 = jnp.float32


def _dot(a, b):
    return jnp.dot(a, b, preferred_element_type=F32)


def _rmsnorm_rows(x, g):
    ms = jnp.mean(x * x, axis=-1, keepdims=True)
    return x * lax.rsqrt(ms + EPS) * g


def _resident(shape):
    return pl.BlockSpec(shape, lambda *_: (0,) * len(shape), pipeline_mode=pl.Buffered(1))


def _map_norm_rope_t(z, gain, cos, sin):
    ss = jnp.sum(z * z, axis=0, keepdims=True)
    zn = z * lax.rsqrt(ss * (1.0 / HEAD_DIM) + EPS) * gain
    half = HEAD_DIM // 2
    x1, x2 = zn[:half], zn[half:]
    return x1 * cos - x2 * sin, x2 * cos + x1 * sin


def _inproj_kernel(x_ref, gmix_ref, wt_ref, wf_ref, gq_ref, gk_ref, cos_ref, sin_ref, dc_ref,
                   qt_ref, k_ref, vt_ref, y1_ref, y2_ref):
    h = _rmsnorm_rows(x_ref[...], gmix_ref[...]).astype(BF16)

    f = _dot(h, wf_ref[...]).astype(BF16)
    dc = dc_ref[...]
    for g in range(N_FOURIER_GROUPS):
        sl = slice(g * FOURIER_GROUP, (g + 1) * FOURIER_GROUP)
        y = _dot(f[:, sl], dc)
        y1_ref[:, sl] = y[:, :FOURIER_GROUP].astype(BF16)
        y2_ref[:, sl] = y[:, FOURIER_GROUP:].astype(BF16)

    h_t = h.T
    cos = cos_ref[...]
    sin = sin_ref[...]
    half = HEAD_DIM // 2

    col_tiles = [slice(t * IN_COLS, (t + 1) * IN_COLS) for t in range(IN_TILE // IN_COLS)]

    def normed_chunk(off, c, gain, toks):
        z_t = _dot(wt_ref[off + c * IN_ROWS:off + (c + 1) * IN_ROWS, :], h_t)
        for tk in toks:
            yield tk, [_map_norm_rope_t(z_t[m * HEAD_DIM:(m + 1) * HEAD_DIM, tk], gain[:, tk],
                                        cos[:, tk], sin[:, tk])
                       for m in range(IN_ROWS // HEAD_DIM)]

    gq = gq_ref[...]
    for c in range(QK_WIDTH // IN_ROWS):
        for tk, maps in normed_chunk(0, c, gq, col_tiles):
            for m, (lo, hi) in enumerate(maps):
                r0 = c * IN_ROWS + m * HEAD_DIM
                qt_ref[0, r0:r0 + half, tk] = lo.astype(BF16)
                qt_ref[0, r0 + half:r0 + HEAD_DIM, tk] = hi.astype(BF16)

    gk = gk_ref[...]
    for c in range(QK_WIDTH // IN_ROWS):
        for tk, maps in normed_chunk(QK_WIDTH, c, gk, col_tiles):
            pieces = [p for lo_hi in maps for p in lo_hi]
            k_ref[tk, c * IN_ROWS:(c + 1) * IN_ROWS] = (
                jnp.concatenate(pieces, axis=0).T.astype(BF16))

    for c in range(V_WIDTH // IN_ROWS):
        rows = slice(c * IN_ROWS, (c + 1) * IN_ROWS)
        vt_ref[0, rows, :] = _dot(wt_ref[2 * QK_WIDTH + c * IN_ROWS:2 * QK_WIDTH + (c + 1) * IN_ROWS, :],
                                  h_t).astype(BF16)


def _inproj(x, gmix, w_qkv_t, w_f, gq, gk, cos_t, sin_t, dc):
    n_tok = x.shape[0]
    n_seq = n_tok // SEQ
    tm = IN_TILE
    pos_blocks = SEQ // tm
    tok = lambda w: pl.BlockSpec((tm, w), lambda i: (i, 0))
    rope = pl.BlockSpec((HEAD_DIM // 2, tm), lambda i: (0, i % pos_blocks))
    dims_by_tok = lambda w: pl.BlockSpec((1, w, tm), lambda i: (i // pos_blocks, 0, i % pos_blocks))
    out_shapes = (
        jax.ShapeDtypeStruct((n_seq, QK_WIDTH, SEQ), BF16),
        jax.ShapeDtypeStruct((n_tok, QK_WIDTH), BF16),
        jax.ShapeDtypeStruct((n_seq, V_WIDTH, SEQ), BF16),
        jax.ShapeDtypeStruct((n_tok, FOURIER_WIDTH), BF16),
        jax.ShapeDtypeStruct((n_tok, FOURIER_WIDTH), BF16),
    )
    return pl.pallas_call(
        _inproj_kernel,
        grid=(n_tok // tm,),
        in_specs=[tok(D_MODEL), _resident((1, D_MODEL)),
                  _resident((2 * QK_WIDTH + V_WIDTH, D_MODEL)), _resident((D_MODEL, FOURIER_WIDTH)),
                  _resident((HEAD_DIM, tm)), _resident((HEAD_DIM, tm)), rope, rope,
                  _resident((FOURIER_GROUP, 2 * FOURIER_GROUP))],
        out_specs=(dims_by_tok(QK_WIDTH), tok(QK_WIDTH), dims_by_tok(V_WIDTH),
                   tok(FOURIER_WIDTH), tok(FOURIER_WIDTH)),
        out_shape=out_shapes,
        compiler_params=pltpu.CompilerParams(
            dimension_semantics=("arbitrary",), vmem_limit_bytes=VMEM_INPROJ),
        name="inproj",
    )(x, gmix, w_qkv_t, w_f, gq, gk, cos_t, sin_t, dc)


def _attn_kernel(bound_ref, lq1_ref, lk1_ref, lq2_ref, lk2_ref, gsub_ref, qt_ref, k_ref, vt_ref,
                 o_ref, lam_ref):
    @pl.when((pl.program_id(0) == 0) & (pl.program_id(1) == 0))
    def _():
        a1 = jnp.sum(lq1_ref[...] * lk1_ref[...], axis=-1, keepdims=True)
        a2 = jnp.sum(lq2_ref[...] * lk2_ref[...], axis=-1, keepdims=True)
        lam_ref[...] = jnp.exp(a1) - jnp.exp(a2) + LAMBDA_INIT

    lam = lam_ref[...]
    gsub = gsub_ref[...]
    tq = Q_TILE
    first_map = lax.broadcasted_iota(jnp.int32, (2 * HEAD_DIM, tq), 0) < HEAD_DIM

    def block(hh, i, shift_by_max):
        head = slice(hh * V_HEAD_DIM, (hh + 1) * V_HEAD_DIM)
        k = k_ref[0, :, head]
        v_t = vt_ref[0, head, :]
        cols = pl.ds(pl.multiple_of(i * tq, tq), tq)
        q_t = qt_ref[0, head, cols]
        zero = jnp.zeros_like(q_t)
        qq_t = jnp.concatenate([jnp.where(first_map, q_t, zero), jnp.where(first_map, zero, q_t)],
                               axis=1)
        s_t = _dot(k, qq_t)
        if shift_by_max:
            s_t = s_t - jnp.max(s_t, axis=0, keepdims=True)
        e = jnp.exp2(s_t)
        inv = 1.0 / jnp.sum(e, axis=0, keepdims=True)
        acc = _dot(v_t, e.astype(BF16))
        o_t = acc[:, :tq] * inv[:, :tq] - acc[:, tq:] * (lam * inv[:, tq:])
        ms = jnp.mean(o_t * o_t, axis=0, keepdims=True)
        o_n = o_t * lax.rsqrt(ms + EPS) * gsub
        o_ref[0, cols, head] = o_n.T.astype(BF16)

    def all_blocks(shift_by_max):
        for hh in range(ATTN_HEADS):
            def body(i, carry):
                block(hh, i, shift_by_max)
                return carry
            lax.fori_loop(0, SEQ // tq, body, 0, unroll=SEQ // tq if not shift_by_max else 1)

    no_shift_needed = bound_ref[0] <= SCORE_BOUND_LIMIT
    pl.when(no_shift_needed)(functools.partial(all_blocks, False))
    pl.when(jnp.logical_not(no_shift_needed))(functools.partial(all_blocks, True))


def _attention(bound, lq1, lk1, lq2, lk2, gsub_t, q_t, k, v_t):
    n_seq = q_t.shape[0]
    width = ATTN_HEADS * V_HEAD_DIM
    head = pl.BlockSpec((1, SEQ, width), lambda b, h: (b, 0, h))
    head_t = pl.BlockSpec((1, width, SEQ), lambda b, h: (b, h, 0))
    return pl.pallas_call(
        _attn_kernel,
        grid=(n_seq, N_HEADS // ATTN_HEADS),
        in_specs=[pl.BlockSpec(memory_space=pltpu.SMEM)] + [_resident((1, HEAD_DIM))] * 4
        + [_resident((V_HEAD_DIM, Q_TILE)), head_t, head, head_t],
        out_specs=head,
        out_shape=jax.ShapeDtypeStruct((n_seq, SEQ, V_WIDTH), BF16),
        scratch_shapes=[pltpu.VMEM((1, 1), F32)],
        compiler_params=pltpu.CompilerParams(
            dimension_semantics=("arbitrary", "arbitrary"), vmem_limit_bytes=VMEM_ATTN),
        name="diff_attn",
    )(bound, lq1, lk1, lq2, lk2, gsub_t, q_t, k, v_t)


def _seq_dft_kernel(cq_ref, sq_ref, jrev_ref, y1_ref, y2_ref, o_ref):
    half = SEQ // 2
    norm = 1.0 / math.sqrt(SEQ * FOURIER_GROUP)
    jrev = jrev_ref[...]
    n_tiles = half // REV_TILE
    row = lax.broadcasted_iota(jnp.int32, (half, 1), 0)

    def reversed_shifted(hi):
        parts = [_dot(jrev, hi[(n_tiles - 1 - t) * REV_TILE:(n_tiles - t) * REV_TILE])
                 for t in range(n_tiles)]
        return pltpu.roll(jnp.concatenate(parts, axis=0), 1, 0)

    not_first = row != 0
    r1 = jnp.where(not_first, reversed_shifted(y1_ref[0, half:, :]), 0.0)
    r2 = jnp.where(not_first, reversed_shifted(y2_ref[0, half:, :]), 0.0)
    p = y1_ref[0, :half, :].astype(F32) + r1
    q = y2_ref[0, :half, :].astype(F32) - r2
    a = _dot(cq_ref[...], p.astype(BF16))
    b = _dot(sq_ref[...], q.astype(BF16))
    alt = jnp.where(row % 2 == 0, 1.0, -1.0)
    y_mid = y1_ref[0, half:half + 1, :].astype(F32) * norm
    corr = alt * y_mid
    o_ref[0, :half, :] = (a - b + corr).astype(BF16)
    upper = reversed_shifted((a + b + corr).astype(BF16))
    mid = norm * jnp.sum(alt * p, axis=0, keepdims=True) + y_mid
    o_ref[0, half:, :] = jnp.where(not_first, upper, mid).astype(BF16)


def _seq_dft(cq, sq, jrev, y1, y2):
    n_seq = y1.shape[0]
    seq = pl.BlockSpec((1, SEQ, FOURIER_WIDTH), lambda b: (b, 0, 0))
    return pl.pallas_call(
        _seq_dft_kernel,
        grid=(n_seq,),
        in_specs=[_resident((SEQ // 2, SEQ // 2)), _resident((SEQ // 2, SEQ // 2)),
                  _resident((REV_TILE, REV_TILE)), seq, seq],
        out_specs=seq,
        out_shape=jax.ShapeDtypeStruct((n_seq, SEQ, FOURIER_WIDTH), BF16),
        compiler_params=pltpu.CompilerParams(
            dimension_semantics=("arbitrary",), vmem_limit_bytes=VMEM_DFT),
        name="seq_dft",
    )(cq, sq, jrev, y1, y2)


def _merge_mlp_kernel(x_ref, att_ref, four_ref, gmix_ref, wgate_ref, bgate_ref, wattn_ref,
                      wfour_ref, wout_ref, gmlp_ref, wup_ref, wdown_ref, y_ref):
    x = x_ref[...]
    h = _rmsnorm_rows(x, gmix_ref[...]).astype(BF16)
    gates = jax.nn.sigmoid(_dot(h, wgate_ref[...]) + bgate_ref[...])
    attn_out = _dot(att_ref[...], wattn_ref[...])
    four_out = _dot(four_ref[...], wfour_ref[...])
    mixed = gates[:, :D_MODEL] * attn_out + gates[:, D_MODEL:] * four_out
    x1 = x + _dot(mixed.astype(BF16), wout_ref[...])
    h2 = _rmsnorm_rows(x1, gmlp_ref[...]).astype(BF16)
    u = jnp.square(jnp.maximum(_dot(h2, wup_ref[...]), 0.0)).astype(BF16)
    y_ref[...] = x1 + _dot(u, wdown_ref[...])


def _merge_mlp(x, att, four, gmix, wgate, bgate, wattn, wfour, wout, gmlp, wup, wdown):
    n_tok = x.shape[0]
    tm = OUT_TILE
    tok = lambda w: pl.BlockSpec((tm, w), lambda i: (i, 0))
    return pl.pallas_call(
        _merge_mlp_kernel,
        grid=(n_tok // tm,),
        in_specs=[tok(D_MODEL), tok(V_WIDTH), tok(FOURIER_WIDTH), _resident((1, D_MODEL)),
                  _resident((D_MODEL, 2 * D_MODEL)), _resident((1, 2 * D_MODEL)),
                  _resident((V_WIDTH, D_MODEL)), _resident((FOURIER_WIDTH, D_MODEL)),
                  _resident((D_MODEL, D_MODEL)), _resident((1, D_MODEL)),
                  _resident((D_MODEL, D_FF)), _resident((D_FF, D_MODEL))],
        out_specs=tok(D_MODEL),
        out_shape=jax.ShapeDtypeStruct((n_tok, D_MODEL), F32),
        compiler_params=pltpu.CompilerParams(
            dimension_semantics=("arbitrary",), vmem_limit_bytes=VMEM_MERGE),
        name="merge_mlp",
    )(x, att, four, gmix, wgate, bgate, wattn, wfour, wout, gmlp, wup, wdown)


def _rope_tables():
    half = HEAD_DIM // 2
    freqs = ROPE_THETA ** (-jnp.arange(half, dtype=F32) * 2.0 / HEAD_DIM)
    ang = jnp.arange(SEQ, dtype=F32)[:, None] * freqs[None, :]
    return jnp.cos(ang).T, jnp.sin(ang).T


def _dft_angles(n):
    idx = jnp.arange(n, dtype=jnp.int32)
    kn = (idx[:, None] * idx[None, :]) % n
    return kn.astype(F32) * (2.0 * math.pi / n)


def _seq_dft_tables(norm):
    half = SEQ // 2
    k = jnp.arange(half, dtype=jnp.int32)[:, None]
    step = 2.0 * math.pi / SEQ
    ang_a = ((k * (jnp.arange(half // LANES, dtype=jnp.int32) * LANES)[None, :]) % SEQ).astype(F32) * step
    ang_b = ((k * jnp.arange(LANES, dtype=jnp.int32)[None, :]) % SEQ).astype(F32) * step
    ca, sa = jnp.cos(ang_a)[:, :, None] * norm, jnp.sin(ang_a)[:, :, None] * norm
    cb, sb = jnp.cos(ang_b)[:, None, :], jnp.sin(ang_b)[:, None, :]
    cq = (ca * cb - sa * sb).reshape(half, half)
    sq = (sa * cb + ca * sb).reshape(half, half)
    return cq.astype(BF16), sq.astype(BF16)


def _dft_tables():
    norm = 1.0 / math.sqrt(SEQ * FOURIER_GROUP)
    cq, sq = _seq_dft_tables(norm)
    jrev = jnp.flip(jnp.eye(REV_TILE, dtype=BF16), axis=0)
    ang_c = _dft_angles(FOURIER_GROUP)
    dc = jnp.concatenate([jnp.cos(ang_c), jnp.sin(ang_c)], axis=1).astype(BF16)
    return cq, sq, jrev, dc


def kernel(x_prompt, x_sample, g_mix, w_in, g_q, g_k, lam_q1, lam_k1, lam_q2, lam_k2, g_sub,
           w_attn_br, w_four_br, w_gate, b_gate, w_out, g_mlp, w_up, w_down):
    row = lambda p: p[0].reshape(1, -1).astype(F32)
    as_bf16 = lambda w: w[0].astype(BF16)
    cos_t, sin_t = _rope_tables()
    cq, sq, jrev, dc = _dft_tables()
    col = lambda p: jnp.broadcast_to(p[0].astype(F32)[:, None], (p.shape[1], IN_TILE))
    gq = col(g_q) * (HEAD_DIM ** -0.5 * math.log2(math.e))
    gk = col(g_k)
    score_bound = (HEAD_DIM * jnp.max(jnp.abs(gq)) * jnp.max(jnp.abs(gk))).reshape(1)
    gsub_t = jnp.broadcast_to(g_sub[0].astype(F32)[:, None] * (1.0 - LAMBDA_INIT),
                              (V_HEAD_DIM, Q_TILE))
    gmix = row(g_mix)
    lams = (row(lam_q1), row(lam_k1), row(lam_q2), row(lam_k2))
    w_in_b, w_gate_b, w_attn_b, w_four_b, w_out_b, w_up_b, w_down_b = map(
        as_bf16, (w_in, w_gate, w_attn_br, w_four_br, w_out, w_up, w_down))
    w_qkv_t = w_in_b[:, :2 * QK_WIDTH + V_WIDTH].T
    w_f = w_in_b[:, 2 * QK_WIDTH + V_WIDTH:]

    def layer(x3):
        n_seq = x3.shape[0]
        x = x3.reshape(n_seq * SEQ, D_MODEL)
        q_t, k, v_t, y1, y2 = _inproj(x, gmix, w_qkv_t, w_f, gq, gk, cos_t, sin_t, dc)
        seq3 = lambda a: a.reshape(n_seq, SEQ, a.shape[-1])
        att = _attention(score_bound, *lams, gsub_t, q_t, seq3(k), v_t)
        four = _seq_dft(cq, sq, jrev, seq3(y1), seq3(y2))
        y = _merge_mlp(x, att.reshape(n_seq * SEQ, V_WIDTH),
                       four.reshape(n_seq * SEQ, FOURIER_WIDTH), gmix, w_gate_b, row(b_gate),
                       w_attn_b, w_four_b, w_out_b, row(g_mlp), w_up_b, w_down_b)
        return y.reshape(n_seq, SEQ, D_MODEL)

    return layer(x_prompt), layer(x_sample)
```

```python
import functools
import math

import jax
import jax.numpy as jnp
from jax import lax
from jax.experimental import pallas as pl
from jax.experimental.pallas import tpu as pltpu

D_MODEL = 1024
SEQ = 2048
N_HEADS = 8
HEAD_DIM = 64
V_HEAD_DIM = 2 * HEAD_DIM
QK_WIDTH = 2 * N_HEADS * HEAD_DIM
V_WIDTH = N_HEADS * V_HEAD_DIM
N_FOURIER_GROUPS = 4
FOURIER_GROUP = 128
FOURIER_WIDTH = N_FOURIER_GROUPS * FOURIER_GROUP
IN_WIDTH = 2 * QK_WIDTH + V_WIDTH + FOURIER_WIDTH
D_FF = 4 * D_MODEL
ROPE_THETA = 10000.0
EPS = 1e-6
LAMBDA_INIT = 0.8 - 0.6 * math.exp(-0.3 * 0)

LANES = 128
IN_TILE = 1024
IN_HALF = 512
IN_COLS = 256
OUT_TILE = 512
Q_TILE = 256
ATTN_HEADS = 2
REV_TILE = 256
SCORE_BOUND_LIMIT = 48.0
MIB = 1024 * 1024
VMEM_INPROJ = 48 * MIB
VMEM_ATTN = 40 * MIB
VMEM_DFT = 48 * MIB
VMEM_MERGE = 56 * MIB

BF16 = jnp.bfloat16
F32 = jnp.float32


def _dot(a, b):
    return jnp.dot(a, b, preferred_element_type=F32)


def _rmsnorm_rows(x, g):
    ms = jnp.mean(x * x, axis=-1, keepdims=True)
    return x * lax.rsqrt(ms + EPS) * g


def _resident(shape):
    return pl.BlockSpec(shape, lambda *_: (0,) * len(shape), pipeline_mode=pl.Buffered(1))


def _map_norm_rope_t(z, gain, cos, sin):
    ss = jnp.sum(z * z, axis=0, keepdims=True)
    zn = z * lax.rsqrt(ss * (1.0 / HEAD_DIM) + EPS) * gain
    half = HEAD_DIM // 2
    x1, x2 = zn[:half], zn[half:]
    return x1 * cos - x2 * sin, x2 * cos + x1 * sin


def _inproj_kernel(x_ref, gmix_ref, w_ref, gq_ref, gk_ref, cos_ref, sin_ref, dc_ref,
                   qt_ref, k_ref, vt_ref, y1_ref, y2_ref):
    half = HEAD_DIM // 2
    dc = dc_ref[...]
    gq = gq_ref[...]
    gk = gk_ref[...]
    v_off = 2 * QK_WIDTH

    def normed_maps(z, gain, cos, sin):
        z_t = z.T
        return [_map_norm_rope_t(z_t[m * HEAD_DIM:(m + 1) * HEAD_DIM], gain, cos, sin)
                for m in range(IN_COLS // HEAD_DIM)]

    for r in range(IN_TILE // IN_HALF):
        rows = slice(r * IN_HALF, (r + 1) * IN_HALF)
        h = _rmsnorm_rows(x_ref[rows, :], gmix_ref[...]).astype(BF16)
        cos = cos_ref[:, rows]
        sin = sin_ref[:, rows]

        z = _dot(h, w_ref[:, :QK_WIDTH])
        for c in range(QK_WIDTH // IN_COLS):
            for m, (lo, hi) in enumerate(normed_maps(z[:, c * IN_COLS:(c + 1) * IN_COLS],
                                                     gq, cos, sin)):
                r0 = c * IN_COLS + m * HEAD_DIM
                qt_ref[0, r0:r0 + half, rows] = lo.astype(BF16)
                qt_ref[0, r0 + half:r0 + HEAD_DIM, rows] = hi.astype(BF16)

        z = _dot(h, w_ref[:, QK_WIDTH:2 * QK_WIDTH])
        for c in range(QK_WIDTH // IN_COLS):
            cols = slice(c * IN_COLS, (c + 1) * IN_COLS)
            pieces = [p for lo_hi in normed_maps(z[:, cols], gk, cos, sin) for p in lo_hi]
            k_ref[rows, cols] = jnp.concatenate(pieces, axis=0).T.astype(BF16)

        vt_ref[0, :, rows] = _dot(h, w_ref[:, v_off:v_off + V_WIDTH]).astype(BF16).T

        f = _dot(h, w_ref[:, v_off + V_WIDTH:IN_WIDTH]).astype(BF16)
        for g in range(N_FOURIER_GROUPS):
            sl = slice(g * FOURIER_GROUP, (g + 1) * FOURIER_GROUP)
            y = _dot(f[:, sl], dc)
            y1_ref[rows, sl] = y[:, :FOURIER_GROUP].astype(BF16)
            y2_ref[rows, sl] = y[:, FOURIER_GROUP:].astype(BF16)


def _inproj(x, gmix, w_in, gq, gk, cos_t, sin_t, dc):
    n_tok = x.shape[0]
    n_seq = n_tok // SEQ
    tm = IN_TILE
    pos_blocks = SEQ // tm
    tok = lambda w: pl.BlockSpec((tm, w), lambda i: (i, 0))
    rope = pl.BlockSpec((HEAD_DIM // 2, tm), lambda i: (0, i % pos_blocks))
    dims_by_tok = lambda w: pl.BlockSpec((1, w, tm), lambda i: (i // pos_blocks, 0, i % pos_blocks))
    out_shapes = (
        jax.ShapeDtypeStruct((n_seq, QK_WIDTH, SEQ), BF16),
        jax.ShapeDtypeStruct((n_tok, QK_WIDTH), BF16),
        jax.ShapeDtypeStruct((n_seq, V_WIDTH, SEQ), BF16),
        jax.ShapeDtypeStruct((n_tok, FOURIER_WIDTH), BF16),
        jax.ShapeDtypeStruct((n_tok, FOURIER_WIDTH), BF16),
    )
    return pl.pallas_call(
        _inproj_kernel,
        grid=(n_tok // tm,),
        in_specs=[tok(D_MODEL), _resident((1, D_MODEL)), _resident((D_MODEL, IN_WIDTH)),
                  _resident((HEAD_DIM, IN_HALF)), _resident((HEAD_DIM, IN_HALF)), rope, rope,
                  _resident((FOURIER_GROUP, 2 * FOURIER_GROUP))],
        out_specs=(dims_by_tok(QK_WIDTH), tok(QK_WIDTH), dims_by_tok(V_WIDTH),
                   tok(FOURIER_WIDTH), tok(FOURIER_WIDTH)),
        out_shape=out_shapes,
        compiler_params=pltpu.CompilerParams(
            dimension_semantics=("arbitrary",), vmem_limit_bytes=VMEM_INPROJ),
        name="inproj",
    )(x, gmix, w_in, gq, gk, cos_t, sin_t, dc)


def _attn_kernel(bound_ref, lq1_ref, lk1_ref, lq2_ref, lk2_ref, gsub_ref, qt_ref, k_ref, vt_ref,
                 o_ref, lam_ref):
    @pl.when((pl.program_id(0) == 0) & (pl.program_id(1) == 0))
    def _():
        a1 = jnp.sum(lq1_ref[...] * lk1_ref[...], axis=-1, keepdims=True)
        a2 = jnp.sum(lq2_ref[...] * lk2_ref[...], axis=-1, keepdims=True)
        lam_ref[...] = jnp.exp(a1) - jnp.exp(a2) + LAMBDA_INIT

    lam = lam_ref[...]
    gsub = gsub_ref[...]
    tq = Q_TILE
    first_map = lax.broadcasted_iota(jnp.int32, (2 * HEAD_DIM, tq), 0) < HEAD_DIM

    def block(hh, i, shift_by_max):
        head = slice(hh * V_HEAD_DIM, (hh + 1) * V_HEAD_DIM)
        k = k_ref[0, :, head]
        v_t = vt_ref[0, head, :]
        cols = pl.ds(pl.multiple_of(i * tq, tq), tq)
        q_t = qt_ref[0, head, cols]
        zero = jnp.zeros_like(q_t)
        qq_t = jnp.concatenate([jnp.where(first_map, q_t, zero), jnp.where(first_map, zero, q_t)],
                               axis=1)
        s_t = _dot(k, qq_t)
        if shift_by_max:
            s_t = s_t - jnp.max(s_t, axis=0, keepdims=True)
        e = jnp.exp2(s_t)
        inv = 1.0 / jnp.sum(e, axis=0, keepdims=True)
        acc = _dot(v_t, e.astype(BF16))
        o_t = acc[:, :tq] * inv[:, :tq] - acc[:, tq:] * (lam * inv[:, tq:])
        ms = jnp.mean(o_t * o_t, axis=0, keepdims=True)
        o_n = o_t * lax.rsqrt(ms + EPS) * gsub
        o_ref[0, cols, head] = o_n.T.astype(BF16)

    def all_blocks(shift_by_max):
        for hh in range(ATTN_HEADS):
            def body(i, carry):
                block(hh, i, shift_by_max)
                return carry
            lax.fori_loop(0, SEQ // tq, body, 0, unroll=SEQ // tq if not shift_by_max else 1)

    no_shift_needed = bound_ref[0] <= SCORE_BOUND_LIMIT
    pl.when(no_shift_needed)(functools.partial(all_blocks, False))
    pl.when(jnp.logical_not(no_shift_needed))(functools.partial(all_blocks, True))


def _attention(bound, lq1, lk1, lq2, lk2, gsub_t, q_t, k, v_t):
    n_seq = q_t.shape[0]
    width = ATTN_HEADS * V_HEAD_DIM
    head = pl.BlockSpec((1, SEQ, width), lambda b, h: (b, 0, h))
    head_t = pl.BlockSpec((1, width, SEQ), lambda b, h: (b, h, 0))
    return pl.pallas_call(
        _attn_kernel,
        grid=(n_seq, N_HEADS // ATTN_HEADS),
        in_specs=[pl.BlockSpec(memory_space=pltpu.SMEM)] + [_resident((1, HEAD_DIM))] * 4
        + [_resident((V_HEAD_DIM, Q_TILE)), head_t, head, head_t],
        out_specs=head,
        out_shape=jax.ShapeDtypeStruct((n_seq, SEQ, V_WIDTH), BF16),
        scratch_shapes=[pltpu.VMEM((1, 1), F32)],
        compiler_params=pltpu.CompilerParams(
            dimension_semantics=("arbitrary", "arbitrary"), vmem_limit_bytes=VMEM_ATTN),
        name="diff_attn",
    )(bound, lq1, lk1, lq2, lk2, gsub_t, q_t, k, v_t)


def _seq_dft_kernel(cq_ref, sq_ref, jrev_ref, y1_ref, y2_ref, o_ref):
    half = SEQ // 2
    norm = 1.0 / math.sqrt(SEQ * FOURIER_GROUP)
    jrev = jrev_ref[...]
    n_tiles = half // REV_TILE
    row = lax.broadcasted_iota(jnp.int32, (half, 1), 0)

    def reversed_shifted(hi):
        parts = [_dot(jrev, hi[(n_tiles - 1 - t) * REV_TILE:(n_tiles - t) * REV_TILE])
                 for t in range(n_tiles)]
        return pltpu.roll(jnp.concatenate(parts, axis=0), 1, 0)

    not_first = row != 0
    r1 = jnp.where(not_first, reversed_shifted(y1_ref[0, half:, :]), 0.0)
    r2 = jnp.where(not_first, reversed_shifted(y2_ref[0, half:, :]), 0.0)
    p = y1_ref[0, :half, :].astype(F32) + r1
    q = y2_ref[0, :half, :].astype(F32) - r2
    a = _dot(cq_ref[...], p.astype(BF16))
    b = _dot(sq_ref[...], q.astype(BF16))
    alt = jnp.where(row % 2 == 0, 1.0, -1.0)
    y_mid = y1_ref[0, half:half + 1, :].astype(F32) * norm
    corr = alt * y_mid
    o_ref[0, :half, :] = (a - b + corr).astype(BF16)
    upper = reversed_shifted((a + b + corr).astype(BF16))
    mid = norm * jnp.sum(alt * p, axis=0, keepdims=True) + y_mid
    o_ref[0, half:, :] = jnp.where(not_first, upper, mid).astype(BF16)


def _seq_dft(cq, sq, jrev, y1, y2):
    n_seq = y1.shape[0]
    seq = pl.BlockSpec((1, SEQ, FOURIER_WIDTH), lambda b: (b, 0, 0))
    return pl.pallas_call(
        _seq_dft_kernel,
        grid=(n_seq,),
        in_specs=[_resident((SEQ // 2, SEQ // 2)), _resident((SEQ // 2, SEQ // 2)),
                  _resident((REV_TILE, REV_TILE)), seq, seq],
        out_specs=seq,
        out_shape=jax.ShapeDtypeStruct((n_seq, SEQ, FOURIER_WIDTH), BF16),
        compiler_params=pltpu.CompilerParams(
            dimension_semantics=("arbitrary",), vmem_limit_bytes=VMEM_DFT),
        name="seq_dft",
    )(cq, sq, jrev, y1, y2)


def _merge_mlp_kernel(x_ref, att_ref, four_ref, gmix_ref, wgate_ref, bgate_ref, wattn_ref,
                      wfour_ref, wout_ref, gmlp_ref, wup_ref, wdown_ref, y_ref):
    x = x_ref[...]
    h = _rmsnorm_rows(x, gmix_ref[...]).astype(BF16)
    gates = jax.nn.sigmoid(_dot(h, wgate_ref[...]) + bgate_ref[...])
    attn_out = _dot(att_ref[...], wattn_ref[...])
    four_out = _dot(four_ref[...], wfour_ref[...])
    mixed = gates[:, :D_MODEL] * attn_out + gates[:, D_MODEL:] * four_out
    x1 = x + _dot(mixed.astype(BF16), wout_ref[...])
    h2 = _rmsnorm_rows(x1, gmlp_ref[...]).astype(BF16)
    u = jnp.square(jnp.maximum(_dot(h2, wup_ref[...]), 0.0)).astype(BF16)
    y_ref[...] = x1 + _dot(u, wdown_ref[...])


def _merge_mlp(x, att, four, gmix, wgate, bgate, wattn, wfour, wout, gmlp, wup, wdown):
    n_tok = x.shape[0]
    tm = OUT_TILE
    tok = lambda w: pl.BlockSpec((tm, w), lambda i: (i, 0))
    return pl.pallas_call(
        _merge_mlp_kernel,
        grid=(n_tok // tm,),
        in_specs=[tok(D_MODEL), tok(V_WIDTH), tok(FOURIER_WIDTH), _resident((1, D_MODEL)),
                  _resident((D_MODEL, 2 * D_MODEL)), _resident((1, 2 * D_MODEL)),
                  _resident((V_WIDTH, D_MODEL)), _resident((FOURIER_WIDTH, D_MODEL)),
                  _resident((D_MODEL, D_MODEL)), _resident((1, D_MODEL)),
                  _resident((D_MODEL, D_FF)), _resident((D_FF, D_MODEL))],
        out_specs=tok(D_MODEL),
        out_shape=jax.ShapeDtypeStruct((n_tok, D_MODEL), F32),
        compiler_params=pltpu.CompilerParams(
            dimension_semantics=("arbitrary",), vmem_limit_bytes=VMEM_MERGE),
        name="merge_mlp",
    )(x, att, four, gmix, wgate, bgate, wattn, wfour, wout, gmlp, wup, wdown)


def _rope_tables():
    half = HEAD_DIM // 2
    freqs = ROPE_THETA ** (-jnp.arange(half, dtype=F32) * 2.0 / HEAD_DIM)
    ang = jnp.arange(SEQ, dtype=F32)[:, None] * freqs[None, :]
    return jnp.cos(ang).T, jnp.sin(ang).T


def _dft_angles(n):
    idx = jnp.arange(n, dtype=jnp.int32)
    kn = (idx[:, None] * idx[None, :]) % n
    return kn.astype(F32) * (2.0 * math.pi / n)


def _seq_dft_tables(norm):
    half = SEQ // 2
    k = jnp.arange(half, dtype=jnp.int32)[:, None]
    step = 2.0 * math.pi / SEQ
    ang_a = ((k * (jnp.arange(half // LANES, dtype=jnp.int32) * LANES)[None, :]) % SEQ).astype(F32) * step
    ang_b = ((k * jnp.arange(LANES, dtype=jnp.int32)[None, :]) % SEQ).astype(F32) * step
    ca, sa = jnp.cos(ang_a)[:, :, None] * norm, jnp.sin(ang_a)[:, :, None] * norm
    cb, sb = jnp.cos(ang_b)[:, None, :], jnp.sin(ang_b)[:, None, :]
    cq = (ca * cb - sa * sb).reshape(half, half)
    sq = (sa * cb + ca * sb).reshape(half, half)
    return cq.astype(BF16), sq.astype(BF16)


def _dft_tables():
    norm = 1.0 / math.sqrt(SEQ * FOURIER_GROUP)
    cq, sq = _seq_dft_tables(norm)
    jrev = jnp.flip(jnp.eye(REV_TILE, dtype=BF16), axis=0)
    ang_c = _dft_angles(FOURIER_GROUP)
    dc = jnp.concatenate([jnp.cos(ang_c), jnp.sin(ang_c)], axis=1).astype(BF16)
    return cq, sq, jrev, dc


def kernel(x_prompt, x_sample, g_mix, w_in, g_q, g_k, lam_q1, lam_k1, lam_q2, lam_k2, g_sub,
           w_attn_br, w_four_br, w_gate, b_gate, w_out, g_mlp, w_up, w_down):
    row = lambda p: p[0].reshape(1, -1).astype(F32)
    as_bf16 = lambda w: w[0].astype(BF16)
    cos_t, sin_t = _rope_tables()
    cq, sq, jrev, dc = _dft_tables()
    col = lambda p: jnp.broadcast_to(p[0].astype(F32)[:, None], (p.shape[1], IN_HALF))
    gq = col(g_q) * (HEAD_DIM ** -0.5 * math.log2(math.e))
    gk = col(g_k)
    score_bound = (HEAD_DIM * jnp.max(jnp.abs(gq)) * jnp.max(jnp.abs(gk))).reshape(1)
    gsub_t = jnp.broadcast_to(g_sub[0].astype(F32)[:, None] * (1.0 - LAMBDA_INIT),
                              (V_HEAD_DIM, Q_TILE))
    gmix = row(g_mix)
    lams = (row(lam_q1), row(lam_k1), row(lam_q2), row(lam_k2))
    w_in_b, w_gate_b, w_attn_b, w_four_b, w_out_b, w_up_b, w_down_b = map(
        as_bf16, (w_in, w_gate, w_attn_br, w_four_br, w_out, w_up, w_down))

    def layer(x3):
        n_seq = x3.shape[0]
        x = x3.reshape(n_seq * SEQ, D_MODEL)
        q_t, k, v_t, y1, y2 = _inproj(x, gmix, w_in_b, gq, gk, cos_t, sin_t, dc)
        seq3 = lambda a: a.reshape(n_seq, SEQ, a.shape[-1])
        att = _attention(score_bound, *lams, gsub_t, q_t, seq3(k), v_t)
        four = _seq_dft(cq, sq, jrev, seq3(y1), seq3(y2))
        y = _merge_mlp(x, att.reshape(n_seq * SEQ, V_WIDTH),
                       four.reshape(n_seq * SEQ, FOURIER_WIDTH), gmix, w_gate_b, row(b_gate),
                       w_attn_b, w_four_b, w_out_b, row(g_mlp), w_up_b, w_down_b)
        return y.reshape(n_seq, SEQ, D_MODEL)

    return layer(x_prompt), layer(x_sample)
```

```python
import functools
import math

import jax
import jax.numpy as jnp
from jax import lax
from jax.experimental import pallas as pl
from jax.experimental.pallas import tpu as pltpu

D_MODEL = 1024
SEQ = 2048
N_HEADS = 8
HEAD_DIM = 64
V_HEAD_DIM = 2 * HEAD_DIM
QK_WIDTH = 2 * N_HEADS * HEAD_DIM
V_WIDTH = N_HEADS * V_HEAD_DIM
N_FOURIER_GROUPS = 4
FOURIER_GROUP = 128
FOURIER_WIDTH = N_FOURIER_GROUPS * FOURIER_GROUP
IN_WIDTH = 2 * QK_WIDTH + V_WIDTH + FOURIER_WIDTH
D_FF = 4 * D_MODEL
ROPE_THETA = 10000.0
EPS = 1e-6
LAMBDA_INIT = 0.8 - 0.6 * math.exp(-0.3 * 0)

LANES = 128
IN_TILE = 1024
IN_HALF = 512
IN_COLS = 256
OUT_TILE = 1024
OUT_HALF = 512
Q_TILE = 512
ATTN_HEADS = 2
REV_TILE = 256
SCORE_BOUND_LIMIT = 48.0
MIB = 1024 * 1024
VMEM_INPROJ = 48 * MIB
VMEM_ATTN = 40 * MIB
VMEM_DFT = 48 * MIB
VMEM_MERGE = 56 * MIB

BF16 = jnp.bfloat16
F32 = jnp.float32


def _dot(a, b):
    return jnp.dot(a, b, preferred_element_type=F32)


def _rmsnorm_rows(x, g):
    ms = jnp.mean(x * x, axis=-1, keepdims=True)
    return x * lax.rsqrt(ms + EPS) * g


def _resident(shape):
    return pl.BlockSpec(shape, lambda *_: (0,) * len(shape), pipeline_mode=pl.Buffered(1))


def _map_norm_rope_t(z, gain, cos, sin):
    ss = jnp.sum(z * z, axis=0, keepdims=True)
    zn = z * lax.rsqrt(ss * (1.0 / HEAD_DIM) + EPS) * gain
    half = HEAD_DIM // 2
    x1, x2 = zn[:half], zn[half:]
    return x1 * cos - x2 * sin, x2 * cos + x1 * sin


def _inproj_kernel(x_ref, gmix_ref, w_ref, gq_ref, gk_ref, cos_ref, sin_ref, dc_ref,
                   qt_ref, k_ref, vt_ref, y1_ref, y2_ref):
    half = HEAD_DIM // 2
    dc = dc_ref[...]
    gq = gq_ref[...]
    gk = gk_ref[...]
    v_off = 2 * QK_WIDTH

    def normed_maps(z, gain, cos, sin):
        z_t = z.T
        return [_map_norm_rope_t(z_t[m * HEAD_DIM:(m + 1) * HEAD_DIM], gain, cos, sin)
                for m in range(IN_COLS // HEAD_DIM)]

    for r in range(IN_TILE // IN_HALF):
        rows = slice(r * IN_HALF, (r + 1) * IN_HALF)
        h = _rmsnorm_rows(x_ref[rows, :], gmix_ref[...]).astype(BF16)
        cos = cos_ref[:, rows]
        sin = sin_ref[:, rows]

        z = _dot(h, w_ref[:, :QK_WIDTH])
        for c in range(QK_WIDTH // IN_COLS):
            for m, (lo, hi) in enumerate(normed_maps(z[:, c * IN_COLS:(c + 1) * IN_COLS],
                                                     gq, cos, sin)):
                r0 = c * IN_COLS + m * HEAD_DIM
                qt_ref[0, r0:r0 + half, rows] = lo.astype(BF16)
                qt_ref[0, r0 + half:r0 + HEAD_DIM, rows] = hi.astype(BF16)

        z = _dot(h, w_ref[:, QK_WIDTH:2 * QK_WIDTH])
        for c in range(QK_WIDTH // IN_COLS):
            cols = slice(c * IN_COLS, (c + 1) * IN_COLS)
            pieces = [p for lo_hi in normed_maps(z[:, cols], gk, cos, sin) for p in lo_hi]
            k_ref[rows, cols] = jnp.concatenate(pieces, axis=0).T.astype(BF16)

        vt_ref[0, :, rows] = _dot(h, w_ref[:, v_off:v_off + V_WIDTH]).astype(BF16).T

        f = _dot(h, w_ref[:, v_off + V_WIDTH:IN_WIDTH]).astype(BF16)
        for g in range(N_FOURIER_GROUPS):
            sl = slice(g * FOURIER_GROUP, (g + 1) * FOURIER_GROUP)
            y = _dot(f[:, sl], dc)
            y1_ref[rows, sl] = y[:, :FOURIER_GROUP].astype(BF16)
            y2_ref[rows, sl] = y[:, FOURIER_GROUP:].astype(BF16)


def _inproj(x, gmix, w_in, gq, gk, cos_t, sin_t, dc):
    n_tok = x.shape[0]
    n_seq = n_tok // SEQ
    tm = IN_TILE
    pos_blocks = SEQ // tm
    tok = lambda w: pl.BlockSpec((tm, w), lambda i: (i, 0))
    rope = pl.BlockSpec((HEAD_DIM // 2, tm), lambda i: (0, i % pos_blocks))
    dims_by_tok = lambda w: pl.BlockSpec((1, w, tm), lambda i: (i // pos_blocks, 0, i % pos_blocks))
    out_shapes = (
        jax.ShapeDtypeStruct((n_seq, QK_WIDTH, SEQ), BF16),
        jax.ShapeDtypeStruct((n_tok, QK_WIDTH), BF16),
        jax.ShapeDtypeStruct((n_seq, V_WIDTH, SEQ), BF16),
        jax.ShapeDtypeStruct((n_tok, FOURIER_WIDTH), BF16),
        jax.ShapeDtypeStruct((n_tok, FOURIER_WIDTH), BF16),
    )
    return pl.pallas_call(
        _inproj_kernel,
        grid=(n_tok // tm,),
        in_specs=[tok(D_MODEL), _resident((1, D_MODEL)), _resident((D_MODEL, IN_WIDTH)),
                  _resident((HEAD_DIM, IN_HALF)), _resident((HEAD_DIM, IN_HALF)), rope, rope,
                  _resident((FOURIER_GROUP, 2 * FOURIER_GROUP))],
        out_specs=(dims_by_tok(QK_WIDTH), tok(QK_WIDTH), dims_by_tok(V_WIDTH),
                   tok(FOURIER_WIDTH), tok(FOURIER_WIDTH)),
        out_shape=out_shapes,
        compiler_params=pltpu.CompilerParams(
            dimension_semantics=("arbitrary",), vmem_limit_bytes=VMEM_INPROJ),
        name="inproj",
    )(x, gmix, w_in, gq, gk, cos_t, sin_t, dc)


def _attn_kernel(bound_ref, lq1_ref, lk1_ref, lq2_ref, lk2_ref, gsub_ref, qt_ref, k_ref, vt_ref,
                 o_ref, lam_ref):
    @pl.when((pl.program_id(0) == 0) & (pl.program_id(1) == 0))
    def _():
        a1 = jnp.sum(lq1_ref[...] * lk1_ref[...], axis=-1, keepdims=True)
        a2 = jnp.sum(lq2_ref[...] * lk2_ref[...], axis=-1, keepdims=True)
        lam_ref[...] = jnp.exp(a1) - jnp.exp(a2) + LAMBDA_INIT

    lam = lam_ref[...]
    gsub = gsub_ref[...]
    tq = Q_TILE
    first_map = lax.broadcasted_iota(jnp.int32, (2 * HEAD_DIM, tq), 0) < HEAD_DIM

    def block(hh, i, shift_by_max):
        head = slice(hh * V_HEAD_DIM, (hh + 1) * V_HEAD_DIM)
        k = k_ref[0, :, head]
        v_t = vt_ref[0, head, :]
        cols = pl.ds(pl.multiple_of(i * tq, tq), tq)
        q_t = qt_ref[0, head, cols]
        zero = jnp.zeros_like(q_t)
        qq_t = jnp.concatenate([jnp.where(first_map, q_t, zero), jnp.where(first_map, zero, q_t)],
                               axis=1)
        s_t = _dot(k, qq_t)
        if shift_by_max:
            s_t = s_t - jnp.max(s_t, axis=0, keepdims=True)
        e = jnp.exp2(s_t)
        inv = 1.0 / jnp.sum(e, axis=0, keepdims=True)
        acc = _dot(v_t, e.astype(BF16))
        o_t = acc[:, :tq] * inv[:, :tq] - acc[:, tq:] * (lam * inv[:, tq:])
        ms = jnp.mean(o_t * o_t, axis=0, keepdims=True)
        o_n = o_t * lax.rsqrt(ms + EPS) * gsub
        o_ref[0, cols, head] = o_n.T.astype(BF16)

    def all_blocks(shift_by_max):
        for hh in range(ATTN_HEADS):
            def body(i, carry):
                block(hh, i, shift_by_max)
                return carry
            lax.fori_loop(0, SEQ // tq, body, 0, unroll=SEQ // tq if not shift_by_max else 1)

    no_shift_needed = bound_ref[0] <= SCORE_BOUND_LIMIT
    pl.when(no_shift_needed)(functools.partial(all_blocks, False))
    pl.when(jnp.logical_not(no_shift_needed))(functools.partial(all_blocks, True))


def _attention(bound, lq1, lk1, lq2, lk2, gsub_t, q_t, k, v_t):
    n_seq = q_t.shape[0]
    width = ATTN_HEADS * V_HEAD_DIM
    head = pl.BlockSpec((1, SEQ, width), lambda b, h: (b, 0, h))
    head_t = pl.BlockSpec((1, width, SEQ), lambda b, h: (b, h, 0))
    return pl.pallas_call(
        _attn_kernel,
        grid=(n_seq, N_HEADS // ATTN_HEADS),
        in_specs=[pl.BlockSpec(memory_space=pltpu.SMEM)] + [_resident((1, HEAD_DIM))] * 4
        + [_resident((V_HEAD_DIM, Q_TILE)), head_t, head, head_t],
        out_specs=head,
        out_shape=jax.ShapeDtypeStruct((n_seq, SEQ, V_WIDTH), BF16),
        scratch_shapes=[pltpu.VMEM((1, 1), F32)],
        compiler_params=pltpu.CompilerParams(
            dimension_semantics=("arbitrary", "arbitrary"), vmem_limit_bytes=VMEM_ATTN),
        name="diff_attn",
    )(bound, lq1, lk1, lq2, lk2, gsub_t, q_t, k, v_t)


def _seq_dft_kernel(cq_ref, sq_ref, jrev_ref, y1_ref, y2_ref, o_ref):
    half = SEQ // 2
    norm = 1.0 / math.sqrt(SEQ * FOURIER_GROUP)
    jrev = jrev_ref[...]
    n_tiles = half // REV_TILE
    row = lax.broadcasted_iota(jnp.int32, (half, 1), 0)

    def reversed_shifted(hi):
        parts = [_dot(jrev, hi[(n_tiles - 1 - t) * REV_TILE:(n_tiles - t) * REV_TILE])
                 for t in range(n_tiles)]
        return pltpu.roll(jnp.concatenate(parts, axis=0), 1, 0)

    not_first = row != 0
    r1 = jnp.where(not_first, reversed_shifted(y1_ref[0, half:, :]), 0.0)
    r2 = jnp.where(not_first, reversed_shifted(y2_ref[0, half:, :]), 0.0)
    p = y1_ref[0, :half, :].astype(F32) + r1
    q = y2_ref[0, :half, :].astype(F32) - r2
    a = _dot(cq_ref[...], p.astype(BF16))
    b = _dot(sq_ref[...], q.astype(BF16))
    alt = jnp.where(row % 2 == 0, 1.0, -1.0)
    y_mid = y1_ref[0, half:half + 1, :].astype(F32) * norm
    corr = alt * y_mid
    o_ref[0, :half, :] = (a - b + corr).astype(BF16)
    upper = reversed_shifted((a + b + corr).astype(BF16))
    mid = norm * jnp.sum(alt * p, axis=0, keepdims=True) + y_mid
    o_ref[0, half:, :] = jnp.where(not_first, upper, mid).astype(BF16)


def _seq_dft(cq, sq, jrev, y1, y2):
    n_seq = y1.shape[0]
    seq = pl.BlockSpec((1, SEQ, FOURIER_WIDTH), lambda b: (b, 0, 0))
    return pl.pallas_call(
        _seq_dft_kernel,
        grid=(n_seq,),
        in_specs=[_resident((SEQ // 2, SEQ // 2)), _resident((SEQ // 2, SEQ // 2)),
                  _resident((REV_TILE, REV_TILE)), seq, seq],
        out_specs=seq,
        out_shape=jax.ShapeDtypeStruct((n_seq, SEQ, FOURIER_WIDTH), BF16),
        compiler_params=pltpu.CompilerParams(
            dimension_semantics=("arbitrary",), vmem_limit_bytes=VMEM_DFT),
        name="seq_dft",
    )(cq, sq, jrev, y1, y2)


def _merge_mlp_kernel(x_ref, att_ref, four_ref, gmix_ref, wgate_ref, bgate_ref, wattn_ref,
                      wfour_ref, wout_ref, gmlp_ref, wup_ref, wdown_ref, y_ref):
    for r in range(OUT_TILE // OUT_HALF):
        rows = slice(r * OUT_HALF, (r + 1) * OUT_HALF)
        x = x_ref[rows, :]
        h = _rmsnorm_rows(x, gmix_ref[...]).astype(BF16)
        gates = jax.nn.sigmoid(_dot(h, wgate_ref[...]) + bgate_ref[...])
        attn_out = _dot(att_ref[rows, :], wattn_ref[...])
        four_out = _dot(four_ref[rows, :], wfour_ref[...])
        mixed = gates[:, :D_MODEL] * attn_out + gates[:, D_MODEL:] * four_out
        x1 = x + _dot(mixed.astype(BF16), wout_ref[...])
        h2 = _rmsnorm_rows(x1, gmlp_ref[...]).astype(BF16)
        u = jnp.square(jnp.maximum(_dot(h2, wup_ref[...]), 0.0)).astype(BF16)
        y_ref[rows, :] = x1 + _dot(u, wdown_ref[...])


def _merge_mlp(x, att, four, gmix, wgate, bgate, wattn, wfour, wout, gmlp, wup, wdown):
    n_tok = x.shape[0]
    tm = OUT_TILE
    tok = lambda w: pl.BlockSpec((tm, w), lambda i: (i, 0))
    return pl.pallas_call(
        _merge_mlp_kernel,
        grid=(n_tok // tm,),
        in_specs=[tok(D_MODEL), tok(V_WIDTH), tok(FOURIER_WIDTH), _resident((1, D_MODEL)),
                  _resident((D_MODEL, 2 * D_MODEL)), _resident((1, 2 * D_MODEL)),
                  _resident((V_WIDTH, D_MODEL)), _resident((FOURIER_WIDTH, D_MODEL)),
                  _resident((D_MODEL, D_MODEL)), _resident((1, D_MODEL)),
                  _resident((D_MODEL, D_FF)), _resident((D_FF, D_MODEL))],
        out_specs=tok(D_MODEL),
        out_shape=jax.ShapeDtypeStruct((n_tok, D_MODEL), F32),
        compiler_params=pltpu.CompilerParams(
            dimension_semantics=("arbitrary",), vmem_limit_bytes=VMEM_MERGE),
        name="merge_mlp",
    )(x, att, four, gmix, wgate, bgate, wattn, wfour, wout, gmlp, wup, wdown)


def _rope_tables():
    half = HEAD_DIM // 2
    freqs = ROPE_THETA ** (-jnp.arange(half, dtype=F32) * 2.0 / HEAD_DIM)
    ang = jnp.arange(SEQ, dtype=F32)[:, None] * freqs[None, :]
    return jnp.cos(ang).T, jnp.sin(ang).T


def _dft_angles(n):
    idx = jnp.arange(n, dtype=jnp.int32)
    kn = (idx[:, None] * idx[None, :]) % n
    return kn.astype(F32) * (2.0 * math.pi / n)


def _seq_dft_tables(norm):
    half = SEQ // 2
    k = jnp.arange(half, dtype=jnp.int32)[:, None]
    step = 2.0 * math.pi / SEQ
    ang_a = ((k * (jnp.arange(half // LANES, dtype=jnp.int32) * LANES)[None, :]) % SEQ).astype(F32) * step
    ang_b = ((k * jnp.arange(LANES, dtype=jnp.int32)[None, :]) % SEQ).astype(F32) * step
    ca, sa = jnp.cos(ang_a)[:, :, None] * norm, jnp.sin(ang_a)[:, :, None] * norm
    cb, sb = jnp.cos(ang_b)[:, None, :], jnp.sin(ang_b)[:, None, :]
    cq = (ca * cb - sa * sb).reshape(half, half)
    sq = (sa * cb + ca * sb).reshape(half, half)
    return cq.astype(BF16), sq.astype(BF16)


def _dft_tables():
    norm = 1.0 / math.sqrt(SEQ * FOURIER_GROUP)
    cq, sq = _seq_dft_tables(norm)
    jrev = jnp.flip(jnp.eye(REV_TILE, dtype=BF16), axis=0)
    ang_c = _dft_angles(FOURIER_GROUP)
    dc = jnp.concatenate([jnp.cos(ang_c), jnp.sin(ang_c)], axis=1).astype(BF16)
    return cq, sq, jrev, dc


def kernel(x_prompt, x_sample, g_mix, w_in, g_q, g_k, lam_q1, lam_k1, lam_q2, lam_k2, g_sub,
           w_attn_br, w_four_br, w_gate, b_gate, w_out, g_mlp, w_up, w_down):
    row = lambda p: p[0].reshape(1, -1).astype(F32)
    as_bf16 = lambda w: w[0].astype(BF16)
    cos_t, sin_t = _rope_tables()
    cq, sq, jrev, dc = _dft_tables()
    col = lambda p: jnp.broadcast_to(p[0].astype(F32)[:, None], (p.shape[1], IN_HALF))
    gq = col(g_q) * (HEAD_DIM ** -0.5 * math.log2(math.e))
    gk = col(g_k)
    score_bound = (HEAD_DIM * jnp.max(jnp.abs(gq)) * jnp.max(jnp.abs(gk))).reshape(1)
    gsub_t = jnp.broadcast_to(g_sub[0].astype(F32)[:, None] * (1.0 - LAMBDA_INIT),
                              (V_HEAD_DIM, Q_TILE))
    gmix = row(g_mix)
    lams = (row(lam_q1), row(lam_k1), row(lam_q2), row(lam_k2))
    w_in_b, w_gate_b, w_attn_b, w_four_b, w_out_b, w_up_b, w_down_b = map(
        as_bf16, (w_in, w_gate, w_attn_br, w_four_br, w_out, w_up, w_down))

    def layer(x3):
        n_seq = x3.shape[0]
        x = x3.reshape(n_seq * SEQ, D_MODEL)
        q_t, k, v_t, y1, y2 = _inproj(x, gmix, w_in_b, gq, gk, cos_t, sin_t, dc)
        seq3 = lambda a: a.reshape(n_seq, SEQ, a.shape[-1])
        att = _attention(score_bound, *lams, gsub_t, q_t, seq3(k), v_t)
        four = _seq_dft(cq, sq, jrev, seq3(y1), seq3(y2))
        y = _merge_mlp(x, att.reshape(n_seq * SEQ, V_WIDTH),
                       four.reshape(n_seq * SEQ, FOURIER_WIDTH), gmix, w_gate_b, row(b_gate),
                       w_attn_b, w_four_b, w_out_b, row(g_mlp), w_up_b, w_down_b)
        return y.reshape(n_seq, SEQ, D_MODEL)

    return layer(x_prompt), layer(x_sample)
```

```python
import functools
import math

import jax
import jax.numpy as jnp
from jax import lax
from jax.experimental import pallas as pl
from jax.experimental.pallas import tpu as pltpu

D_MODEL = 1024
SEQ = 2048
N_HEADS = 8
HEAD_DIM = 64
V_HEAD_DIM = 2 * HEAD_DIM
QK_WIDTH = 2 * N_HEADS * HEAD_DIM
V_WIDTH = N_HEADS * V_HEAD_DIM
N_FOURIER_GROUPS = 4
FOURIER_GROUP = 128
FOURIER_WIDTH = N_FOURIER_GROUPS * FOURIER_GROUP
IN_WIDTH = 2 * QK_WIDTH + V_WIDTH + FOURIER_WIDTH
D_FF = 4 * D_MODEL
ROPE_THETA = 10000.0
EPS = 1e-6
LAMBDA_INIT = 0.8 - 0.6 * math.exp(-0.3 * 0)

LANES = 128
IN_TILE = 1024
IN_HALF = 512
IN_COLS = 256
OUT_TILE = 1024
OUT_HALF = 512
Q_TILE = 1024
ATTN_HEADS = 2
DFT_SEQS = 2
REV_TILE = 256
SCORE_BOUND_LIMIT = 48.0
MIB = 1024 * 1024
VMEM_INPROJ = 48 * MIB
VMEM_ATTN = 40 * MIB
VMEM_DFT = 48 * MIB
VMEM_MERGE = 56 * MIB

BF16 = jnp.bfloat16
F32 = jnp.float32


def _dot(a, b):
    return jnp.dot(a, b, preferred_element_type=F32)


def _rmsnorm_rows(x, g):
    ms = jnp.mean(x * x, axis=-1, keepdims=True)
    return x * lax.rsqrt(ms + EPS) * g


def _resident(shape):
    return pl.BlockSpec(shape, lambda *_: (0,) * len(shape), pipeline_mode=pl.Buffered(1))


def _map_norm_rope_t(z, gain, cos, sin):
    ss = jnp.sum(z * z, axis=0, keepdims=True)
    zn = z * lax.rsqrt(ss * (1.0 / HEAD_DIM) + EPS) * gain
    half = HEAD_DIM // 2
    x1, x2 = zn[:half], zn[half:]
    return x1 * cos - x2 * sin, x2 * cos + x1 * sin


def _inproj_kernel(x_ref, gmix_ref, w_ref, gq_ref, gk_ref, cos_ref, sin_ref, dc_ref,
                   qt_ref, k_ref, vt_ref, y1_ref, y2_ref):
    half = HEAD_DIM // 2
    dc = dc_ref[...]
    gq = gq_ref[...]
    gk = gk_ref[...]
    v_off = 2 * QK_WIDTH

    def normed_maps(z, gain, cos, sin):
        z_t = z.T
        return [_map_norm_rope_t(z_t[m * HEAD_DIM:(m + 1) * HEAD_DIM], gain, cos, sin)
                for m in range(IN_COLS // HEAD_DIM)]

    for r in range(IN_TILE // IN_HALF):
        rows = slice(r * IN_HALF, (r + 1) * IN_HALF)
        h = _rmsnorm_rows(x_ref[rows, :], gmix_ref[...]).astype(BF16)
        cos = cos_ref[:, rows]
        sin = sin_ref[:, rows]

        z = _dot(h, w_ref[:, :QK_WIDTH])
        for c in range(QK_WIDTH // IN_COLS):
            for m, (lo, hi) in enumerate(normed_maps(z[:, c * IN_COLS:(c + 1) * IN_COLS],
                                                     gq, cos, sin)):
                r0 = c * IN_COLS + m * HEAD_DIM
                qt_ref[0, r0:r0 + half, rows] = lo.astype(BF16)
                qt_ref[0, r0 + half:r0 + HEAD_DIM, rows] = hi.astype(BF16)

        z = _dot(h, w_ref[:, QK_WIDTH:2 * QK_WIDTH])
        for c in range(QK_WIDTH // IN_COLS):
            cols = slice(c * IN_COLS, (c + 1) * IN_COLS)
            pieces = [p for lo_hi in normed_maps(z[:, cols], gk, cos, sin) for p in lo_hi]
            k_ref[rows, cols] = jnp.concatenate(pieces, axis=0).T.astype(BF16)

        vt_ref[0, :, rows] = _dot(h, w_ref[:, v_off:v_off + V_WIDTH]).astype(BF16).T

        f = _dot(h, w_ref[:, v_off + V_WIDTH:IN_WIDTH]).astype(BF16)
        for g in range(N_FOURIER_GROUPS):
            sl = slice(g * FOURIER_GROUP, (g + 1) * FOURIER_GROUP)
            y = _dot(f[:, sl], dc)
            y1_ref[rows, sl] = y[:, :FOURIER_GROUP].astype(BF16)
            y2_ref[rows, sl] = y[:, FOURIER_GROUP:].astype(BF16)


def _inproj(x, gmix, w_in, gq, gk, cos_t, sin_t, dc):
    n_tok = x.shape[0]
    n_seq = n_tok // SEQ
    tm = IN_TILE
    pos_blocks = SEQ // tm
    tok = lambda w: pl.BlockSpec((tm, w), lambda i: (i, 0))
    rope = pl.BlockSpec((HEAD_DIM // 2, tm), lambda i: (0, i % pos_blocks))
    dims_by_tok = lambda w: pl.BlockSpec((1, w, tm), lambda i: (i // pos_blocks, 0, i % pos_blocks))
    out_shapes = (
        jax.ShapeDtypeStruct((n_seq, QK_WIDTH, SEQ), BF16),
        jax.ShapeDtypeStruct((n_tok, QK_WIDTH), BF16),
        jax.ShapeDtypeStruct((n_seq, V_WIDTH, SEQ), BF16),
        jax.ShapeDtypeStruct((n_tok, FOURIER_WIDTH), BF16),
        jax.ShapeDtypeStruct((n_tok, FOURIER_WIDTH), BF16),
    )
    return pl.pallas_call(
        _inproj_kernel,
        grid=(n_tok // tm,),
        in_specs=[tok(D_MODEL), _resident((1, D_MODEL)), _resident((D_MODEL, IN_WIDTH)),
                  _resident((HEAD_DIM, IN_HALF)), _resident((HEAD_DIM, IN_HALF)), rope, rope,
                  _resident((FOURIER_GROUP, 2 * FOURIER_GROUP))],
        out_specs=(dims_by_tok(QK_WIDTH), tok(QK_WIDTH), dims_by_tok(V_WIDTH),
                   tok(FOURIER_WIDTH), tok(FOURIER_WIDTH)),
        out_shape=out_shapes,
        compiler_params=pltpu.CompilerParams(
            dimension_semantics=("arbitrary",), vmem_limit_bytes=VMEM_INPROJ),
        name="inproj",
    )(x, gmix, w_in, gq, gk, cos_t, sin_t, dc)


def _attn_kernel(bound_ref, lq1_ref, lk1_ref, lq2_ref, lk2_ref, gsub_ref, qt_ref, k_ref, vt_ref,
                 o_ref, lam_ref):
    @pl.when((pl.program_id(0) == 0) & (pl.program_id(1) == 0))
    def _():
        a1 = jnp.sum(lq1_ref[...] * lk1_ref[...], axis=-1, keepdims=True)
        a2 = jnp.sum(lq2_ref[...] * lk2_ref[...], axis=-1, keepdims=True)
        lam_ref[...] = jnp.exp(a1) - jnp.exp(a2) + LAMBDA_INIT

    lam = lam_ref[...]
    gsub = gsub_ref[...]
    tq = Q_TILE
    first_map = lax.broadcasted_iota(jnp.int32, (2 * HEAD_DIM, tq), 0) < HEAD_DIM

    def block(hh, i, shift_by_max):
        head = slice(hh * V_HEAD_DIM, (hh + 1) * V_HEAD_DIM)
        k = k_ref[0, :, head]
        v_t = vt_ref[0, head, :]
        cols = pl.ds(pl.multiple_of(i * tq, tq), tq)
        q_t = qt_ref[0, head, cols]
        zero = jnp.zeros_like(q_t)
        qq_t = jnp.concatenate([jnp.where(first_map, q_t, zero), jnp.where(first_map, zero, q_t)],
                               axis=1)
        s_t = _dot(k, qq_t)
        if shift_by_max:
            s_t = s_t - jnp.max(s_t, axis=0, keepdims=True)
        e = jnp.exp2(s_t)
        inv = 1.0 / jnp.sum(e, axis=0, keepdims=True)
        acc = _dot(v_t, e.astype(BF16))
        o_t = acc[:, :tq] * inv[:, :tq] - acc[:, tq:] * (lam * inv[:, tq:])
        ms = jnp.mean(o_t * o_t, axis=0, keepdims=True)
        o_n = o_t * lax.rsqrt(ms + EPS) * gsub
        o_ref[0, cols, head] = o_n.T.astype(BF16)

    def all_blocks(shift_by_max):
        for hh in range(ATTN_HEADS):
            def body(i, carry):
                block(hh, i, shift_by_max)
                return carry
            lax.fori_loop(0, SEQ // tq, body, 0, unroll=SEQ // tq if not shift_by_max else 1)

    no_shift_needed = bound_ref[0] <= SCORE_BOUND_LIMIT
    pl.when(no_shift_needed)(functools.partial(all_blocks, False))
    pl.when(jnp.logical_not(no_shift_needed))(functools.partial(all_blocks, True))


def _attention(bound, lq1, lk1, lq2, lk2, gsub_t, q_t, k, v_t):
    n_seq = q_t.shape[0]
    width = ATTN_HEADS * V_HEAD_DIM
    head = pl.BlockSpec((1, SEQ, width), lambda b, h: (b, 0, h))
    head_t = pl.BlockSpec((1, width, SEQ), lambda b, h: (b, h, 0))
    return pl.pallas_call(
        _attn_kernel,
        grid=(n_seq, N_HEADS // ATTN_HEADS),
        in_specs=[pl.BlockSpec(memory_space=pltpu.SMEM)] + [_resident((1, HEAD_DIM))] * 4
        + [_resident((V_HEAD_DIM, Q_TILE)), head_t, head, head_t],
        out_specs=head,
        out_shape=jax.ShapeDtypeStruct((n_seq, SEQ, V_WIDTH), BF16),
        scratch_shapes=[pltpu.VMEM((1, 1), F32)],
        compiler_params=pltpu.CompilerParams(
            dimension_semantics=("arbitrary", "arbitrary"), vmem_limit_bytes=VMEM_ATTN),
        name="diff_attn",
    )(bound, lq1, lk1, lq2, lk2, gsub_t, q_t, k, v_t)


def _seq_dft_kernel(cq_ref, sq_ref, jrev_ref, y1_ref, y2_ref, o_ref):
    half = SEQ // 2
    norm = 1.0 / math.sqrt(SEQ * FOURIER_GROUP)
    jrev = jrev_ref[...]
    n_tiles = half // REV_TILE
    row = lax.broadcasted_iota(jnp.int32, (half, 1), 0)

    def reversed_shifted(hi):
        parts = [_dot(jrev, hi[(n_tiles - 1 - t) * REV_TILE:(n_tiles - t) * REV_TILE])
                 for t in range(n_tiles)]
        return pltpu.roll(jnp.concatenate(parts, axis=0), 1, 0)

    not_first = row != 0
    alt = jnp.where(row % 2 == 0, 1.0, -1.0)
    for s in range(DFT_SEQS):
        r1 = jnp.where(not_first, reversed_shifted(y1_ref[s, half:, :]), 0.0)
        r2 = jnp.where(not_first, reversed_shifted(y2_ref[s, half:, :]), 0.0)
        p = y1_ref[s, :half, :].astype(F32) + r1
        q = y2_ref[s, :half, :].astype(F32) - r2
        a = _dot(cq_ref[...], p.astype(BF16))
        b = _dot(sq_ref[...], q.astype(BF16))
        y_mid = y1_ref[s, half:half + 1, :].astype(F32) * norm
        corr = alt * y_mid
        o_ref[s, :half, :] = (a - b + corr).astype(BF16)
        upper = reversed_shifted((a + b + corr).astype(BF16))
        mid = norm * jnp.sum(alt * p, axis=0, keepdims=True) + y_mid
        o_ref[s, half:, :] = jnp.where(not_first, upper, mid).astype(BF16)


def _seq_dft(cq, sq, jrev, y1, y2):
    n_seq = y1.shape[0]
    seq = pl.BlockSpec((DFT_SEQS, SEQ, FOURIER_WIDTH), lambda b: (b, 0, 0))
    return pl.pallas_call(
        _seq_dft_kernel,
        grid=(n_seq // DFT_SEQS,),
        in_specs=[_resident((SEQ // 2, SEQ // 2)), _resident((SEQ // 2, SEQ // 2)),
                  _resident((REV_TILE, REV_TILE)), seq, seq],
        out_specs=seq,
        out_shape=jax.ShapeDtypeStruct((n_seq, SEQ, FOURIER_WIDTH), BF16),
        compiler_params=pltpu.CompilerParams(
            dimension_semantics=("arbitrary",), vmem_limit_bytes=VMEM_DFT),
        name="seq_dft",
    )(cq, sq, jrev, y1, y2)


def _merge_mlp_kernel(x_ref, att_ref, four_ref, gmix_ref, wgate_ref, bgate_ref, wattn_ref,
                      wfour_ref, wout_ref, gmlp_ref, wup_ref, wdown_ref, y_ref):
    for r in range(OUT_TILE // OUT_HALF):
        rows = slice(r * OUT_HALF, (r + 1) * OUT_HALF)
        x = x_ref[rows, :]
        h = _rmsnorm_rows(x, gmix_ref[...]).astype(BF16)
        gates = jax.nn.sigmoid(_dot(h, wgate_ref[...]) + bgate_ref[...])
        attn_out = _dot(att_ref[rows, :], wattn_ref[...])
        four_out = _dot(four_ref[rows, :], wfour_ref[...])
        mixed = gates[:, :D_MODEL] * attn_out + gates[:, D_MODEL:] * four_out
        x1 = x + _dot(mixed.astype(BF16), wout_ref[...])
        h2 = _rmsnorm_rows(x1, gmlp_ref[...]).astype(BF16)
        u = jnp.square(jnp.maximum(_dot(h2, wup_ref[...]), 0.0)).astype(BF16)
        y_ref[rows, :] = x1 + _dot(u, wdown_ref[...])


def _merge_mlp(x, att, four, gmix, wgate, bgate, wattn, wfour, wout, gmlp, wup, wdown):
    n_tok = x.shape[0]
    tm = OUT_TILE
    tok = lambda w: pl.BlockSpec((tm, w), lambda i: (i, 0))
    return pl.pallas_call(
        _merge_mlp_kernel,
        grid=(n_tok // tm,),
        in_specs=[tok(D_MODEL), tok(V_WIDTH), tok(FOURIER_WIDTH), _resident((1, D_MODEL)),
                  _resident((D_MODEL, 2 * D_MODEL)), _resident((1, 2 * D_MODEL)),
                  _resident((V_WIDTH, D_MODEL)), _resident((FOURIER_WIDTH, D_MODEL)),
                  _resident((D_MODEL, D_MODEL)), _resident((1, D_MODEL)),
                  _resident((D_MODEL, D_FF)), _resident((D_FF, D_MODEL))],
        out_specs=tok(D_MODEL),
        out_shape=jax.ShapeDtypeStruct((n_tok, D_MODEL), F32),
        compiler_params=pltpu.CompilerParams(
            dimension_semantics=("arbitrary",), vmem_limit_bytes=VMEM_MERGE),
        name="merge_mlp",
    )(x, att, four, gmix, wgate, bgate, wattn, wfour, wout, gmlp, wup, wdown)


def _rope_tables():
    half = HEAD_DIM // 2
    freqs = ROPE_THETA ** (-jnp.arange(half, dtype=F32) * 2.0 / HEAD_DIM)
    ang = jnp.arange(SEQ, dtype=F32)[:, None] * freqs[None, :]
    return jnp.cos(ang).T, jnp.sin(ang).T


def _dft_angles(n):
    idx = jnp.arange(n, dtype=jnp.int32)
    kn = (idx[:, None] * idx[None, :]) % n
    return kn.astype(F32) * (2.0 * math.pi / n)


def _seq_dft_tables(norm):
    half = SEQ // 2
    k = jnp.arange(half, dtype=jnp.int32)[:, None]
    step = 2.0 * math.pi / SEQ
    ang_a = ((k * (jnp.arange(half // LANES, dtype=jnp.int32) * LANES)[None, :]) % SEQ).astype(F32) * step
    ang_b = ((k * jnp.arange(LANES, dtype=jnp.int32)[None, :]) % SEQ).astype(F32) * step
    ca, sa = jnp.cos(ang_a)[:, :, None] * norm, jnp.sin(ang_a)[:, :, None] * norm
    cb, sb = jnp.cos(ang_b)[:, None, :], jnp.sin(ang_b)[:, None, :]
    cq = (ca * cb - sa * sb).reshape(half, half)
    sq = (sa * cb + ca * sb).reshape(half, half)
    return cq.astype(BF16), sq.astype(BF16)


def _dft_tables():
    norm = 1.0 / math.sqrt(SEQ * FOURIER_GROUP)
    cq, sq = _seq_dft_tables(norm)
    jrev = jnp.flip(jnp.eye(REV_TILE, dtype=BF16), axis=0)
    ang_c = _dft_angles(FOURIER_GROUP)
    dc = jnp.concatenate([jnp.cos(ang_c), jnp.sin(ang_c)], axis=1).astype(BF16)
    return cq, sq, jrev, dc


def kernel(x_prompt, x_sample, g_mix, w_in, g_q, g_k, lam_q1, lam_k1, lam_q2, lam_k2, g_sub,
           w_attn_br, w_four_br, w_gate, b_gate, w_out, g_mlp, w_up, w_down):
    row = lambda p: p[0].reshape(1, -1).astype(F32)
    as_bf16 = lambda w: w[0].astype(BF16)
    cos_t, sin_t = _rope_tables()
    cq, sq, jrev, dc = _dft_tables()
    col = lambda p: jnp.broadcast_to(p[0].astype(F32)[:, None], (p.shape[1], IN_HALF))
    gq = col(g_q) * (HEAD_DIM ** -0.5 * math.log2(math.e))
    gk = col(g_k)
    score_bound = (HEAD_DIM * jnp.max(jnp.abs(gq)) * jnp.max(jnp.abs(gk))).reshape(1)
    gsub_t = jnp.broadcast_to(g_sub[0].astype(F32)[:, None] * (1.0 - LAMBDA_INIT),
                              (V_HEAD_DIM, Q_TILE))
    gmix = row(g_mix)
    lams = (row(lam_q1), row(lam_k1), row(lam_q2), row(lam_k2))
    w_in_b, w_gate_b, w_attn_b, w_four_b, w_out_b, w_up_b, w_down_b = map(
        as_bf16, (w_in, w_gate, w_attn_br, w_four_br, w_out, w_up, w_down))

    def layer(x3):
        n_seq = x3.shape[0]
        x = x3.reshape(n_seq * SEQ, D_MODEL)
        q_t, k, v_t, y1, y2 = _inproj(x, gmix, w_in_b, gq, gk, cos_t, sin_t, dc)
        seq3 = lambda a: a.reshape(n_seq, SEQ, a.shape[-1])
        att = _attention(score_bound, *lams, gsub_t, q_t, seq3(k), v_t)
        four = _seq_dft(cq, sq, jrev, seq3(y1), seq3(y2))
        y = _merge_mlp(x, att.reshape(n_seq * SEQ, V_WIDTH),
                       four.reshape(n_seq * SEQ, FOURIER_WIDTH), gmix, w_gate_b, row(b_gate),
                       w_attn_b, w_four_b, w_out_b, row(g_mlp), w_up_b, w_down_b)
        return y.reshape(n_seq, SEQ, D_MODEL)

    return layer(x_prompt), layer(x_sample)
```

```python
import functools
import math

import jax
import jax.numpy as jnp
from jax import lax
from jax.experimental import pallas as pl
from jax.experimental.pallas import tpu as pltpu

D_MODEL = 1024
SEQ = 2048
N_HEADS = 8
HEAD_DIM = 64
V_HEAD_DIM = 2 * HEAD_DIM
QK_WIDTH = 2 * N_HEADS * HEAD_DIM
V_WIDTH = N_HEADS * V_HEAD_DIM
N_FOURIER_GROUPS = 4
FOURIER_GROUP = 128
FOURIER_WIDTH = N_FOURIER_GROUPS * FOURIER_GROUP
IN_WIDTH = 2 * QK_WIDTH + V_WIDTH + FOURIER_WIDTH
D_FF = 4 * D_MODEL
ROPE_THETA = 10000.0
EPS = 1e-6
LAMBDA_INIT = 0.8 - 0.6 * math.exp(-0.3 * 0)

LANES = 128
IN_TILE = 1024
IN_HALF = 512
IN_COLS = 256
OUT_TILE = 1024
OUT_HALF = 512
Q_TILE = 512
ATTN_HEADS = 2
REV_TILE = 256
SCORE_BOUND_LIMIT = 48.0
MIB = 1024 * 1024
VMEM_INPROJ = 48 * MIB
VMEM_ATTN = 40 * MIB
VMEM_DFT = 48 * MIB
VMEM_MERGE = 56 * MIB

BF16 = jnp.bfloat16
F32 = jnp.float32


def _dot(a, b):
    return jnp.dot(a, b, preferred_element_type=F32)


def _rmsnorm_rows(x, g):
    ms = jnp.mean(x * x, axis=-1, keepdims=True)
    return x * lax.rsqrt(ms + EPS) * g


def _resident(shape):
    return pl.BlockSpec(shape, lambda *_: (0,) * len(shape), pipeline_mode=pl.Buffered(1))


def _map_norm_rope_t(z, gain, cos, sin):
    ss = jnp.sum(z * z, axis=0, keepdims=True)
    zn = z * lax.rsqrt(ss * (1.0 / HEAD_DIM) + EPS) * gain
    half = HEAD_DIM // 2
    x1, x2 = zn[:half], zn[half:]
    return x1 * cos - x2 * sin, x2 * cos + x1 * sin


def _inproj_kernel(x_ref, gmix_ref, w_ref, gq_ref, gk_ref, cos_ref, sin_ref, dc_ref,
                   qt_ref, k_ref, vt_ref, y1_ref, y2_ref):
    half = HEAD_DIM // 2
    dc = dc_ref[...]
    gq = gq_ref[...]
    gk = gk_ref[...]
    v_off = 2 * QK_WIDTH

    def normed_maps(z, gain, cos, sin):
        z_t = z.T
        return [_map_norm_rope_t(z_t[m * HEAD_DIM:(m + 1) * HEAD_DIM], gain, cos, sin)
                for m in range(IN_COLS // HEAD_DIM)]

    for r in range(IN_TILE // IN_HALF):
        rows = slice(r * IN_HALF, (r + 1) * IN_HALF)
        h = _rmsnorm_rows(x_ref[rows, :], gmix_ref[...]).astype(BF16)
        cos = cos_ref[:, rows]
        sin = sin_ref[:, rows]

        z = _dot(h, w_ref[:, :QK_WIDTH])
        for c in range(QK_WIDTH // IN_COLS):
            for m, (lo, hi) in enumerate(normed_maps(z[:, c * IN_COLS:(c + 1) * IN_COLS],
                                                     gq, cos, sin)):
                r0 = c * IN_COLS + m * HEAD_DIM
                qt_ref[0, r0:r0 + half, rows] = lo.astype(BF16)
                qt_ref[0, r0 + half:r0 + HEAD_DIM, rows] = hi.astype(BF16)

        z = _dot(h, w_ref[:, QK_WIDTH:2 * QK_WIDTH])
        for c in range(QK_WIDTH // IN_COLS):
            cols = slice(c * IN_COLS, (c + 1) * IN_COLS)
            pieces = [p for lo_hi in normed_maps(z[:, cols], gk, cos, sin) for p in lo_hi]
            k_ref[rows, cols] = jnp.concatenate(pieces, axis=0).T.astype(BF16)

        vt_ref[0, :, rows] = _dot(h, w_ref[:, v_off:v_off + V_WIDTH]).astype(BF16).T

        f = _dot(h, w_ref[:, v_off + V_WIDTH:IN_WIDTH]).astype(BF16)
        for g in range(N_FOURIER_GROUPS):
            sl = slice(g * FOURIER_GROUP, (g + 1) * FOURIER_GROUP)
            y = _dot(f[:, sl], dc)
            y1_ref[rows, sl] = y[:, :FOURIER_GROUP].astype(BF16)
            y2_ref[rows, sl] = y[:, FOURIER_GROUP:].astype(BF16)


def _inproj(x, gmix, w_in, gq, gk, cos_t, sin_t, dc):
    n_tok = x.shape[0]
    n_seq = n_tok // SEQ
    tm = IN_TILE
    pos_blocks = SEQ // tm
    tok = lambda w: pl.BlockSpec((tm, w), lambda i: (i, 0))
    rope = pl.BlockSpec((HEAD_DIM // 2, tm), lambda i: (0, i % pos_blocks))
    dims_by_tok = lambda w: pl.BlockSpec((1, w, tm), lambda i: (i // pos_blocks, 0, i % pos_blocks))
    out_shapes = (
        jax.ShapeDtypeStruct((n_seq, QK_WIDTH, SEQ), BF16),
        jax.ShapeDtypeStruct((n_tok, QK_WIDTH), BF16),
        jax.ShapeDtypeStruct((n_seq, V_WIDTH, SEQ), BF16),
        jax.ShapeDtypeStruct((n_tok, FOURIER_WIDTH), BF16),
        jax.ShapeDtypeStruct((n_tok, FOURIER_WIDTH), BF16),
    )
    return pl.pallas_call(
        _inproj_kernel,
        grid=(n_tok // tm,),
        in_specs=[tok(D_MODEL), _resident((1, D_MODEL)), _resident((D_MODEL, IN_WIDTH)),
                  _resident((HEAD_DIM, IN_HALF)), _resident((HEAD_DIM, IN_HALF)), rope, rope,
                  _resident((FOURIER_GROUP, 2 * FOURIER_GROUP))],
        out_specs=(dims_by_tok(QK_WIDTH), tok(QK_WIDTH), dims_by_tok(V_WIDTH),
                   tok(FOURIER_WIDTH), tok(FOURIER_WIDTH)),
        out_shape=out_shapes,
        compiler_params=pltpu.CompilerParams(
            dimension_semantics=("arbitrary",), vmem_limit_bytes=VMEM_INPROJ),
        name="inproj",
    )(x, gmix, w_in, gq, gk, cos_t, sin_t, dc)


def _attn_kernel(bound_ref, lq1_ref, lk1_ref, lq2_ref, lk2_ref, gsub_ref, qt_ref, k_ref, vt_ref,
                 o_ref, lam_ref):
    @pl.when((pl.program_id(0) == 0) & (pl.program_id(1) == 0))
    def _():
        a1 = jnp.sum(lq1_ref[...] * lk1_ref[...], axis=-1, keepdims=True)
        a2 = jnp.sum(lq2_ref[...] * lk2_ref[...], axis=-1, keepdims=True)
        lam_ref[...] = jnp.exp(a1) - jnp.exp(a2) + LAMBDA_INIT

    lam = lam_ref[...]
    gsub = gsub_ref[...]
    tq = Q_TILE
    first_map = lax.broadcasted_iota(jnp.int32, (2 * HEAD_DIM, tq), 0) < HEAD_DIM

    def block(hh, i, shift_by_max):
        head = slice(hh * V_HEAD_DIM, (hh + 1) * V_HEAD_DIM)
        k = k_ref[0, :, head]
        v_t = vt_ref[0, head, :]
        cols = pl.ds(pl.multiple_of(i * tq, tq), tq)
        q_t = qt_ref[0, head, cols]
        zero = jnp.zeros_like(q_t)
        qq_t = jnp.concatenate([jnp.where(first_map, q_t, zero), jnp.where(first_map, zero, q_t)],
                               axis=1)
        s_t = _dot(k, qq_t)
        if shift_by_max:
            s_t = s_t - jnp.max(s_t, axis=0, keepdims=True)
        e = jnp.exp2(s_t)
        inv = 1.0 / jnp.sum(e, axis=0, keepdims=True)
        acc = _dot(v_t, e.astype(BF16))
        o_t = acc[:, :tq] * inv[:, :tq] - acc[:, tq:] * (lam * inv[:, tq:])
        ms = jnp.mean(o_t * o_t, axis=0, keepdims=True)
        o_n = o_t * lax.rsqrt(ms + EPS) * gsub
        o_ref[0, cols, head] = o_n.T.astype(BF16)

    def all_blocks(shift_by_max):
        for hh in range(ATTN_HEADS):
            def body(i, carry):
                block(hh, i, shift_by_max)
                return carry
            lax.fori_loop(0, SEQ // tq, body, 0, unroll=SEQ // tq if not shift_by_max else 1)

    no_shift_needed = bound_ref[0] <= SCORE_BOUND_LIMIT
    pl.when(no_shift_needed)(functools.partial(all_blocks, False))
    pl.when(jnp.logical_not(no_shift_needed))(functools.partial(all_blocks, True))


def _attention(bound, lq1, lk1, lq2, lk2, gsub_t, q_t, k, v_t):
    n_seq = q_t.shape[0]
    width = ATTN_HEADS * V_HEAD_DIM
    head = pl.BlockSpec((1, SEQ, width), lambda b, h: (b, 0, h))
    head_t = pl.BlockSpec((1, width, SEQ), lambda b, h: (b, h, 0))
    return pl.pallas_call(
        _attn_kernel,
        grid=(n_seq, N_HEADS // ATTN_HEADS),
        in_specs=[pl.BlockSpec(memory_space=pltpu.SMEM)] + [_resident((1, HEAD_DIM))] * 4
        + [_resident((V_HEAD_DIM, Q_TILE)), head_t, head, head_t],
        out_specs=head,
        out_shape=jax.ShapeDtypeStruct((n_seq, SEQ, V_WIDTH), BF16),
        scratch_shapes=[pltpu.VMEM((1, 1), F32)],
        compiler_params=pltpu.CompilerParams(
            dimension_semantics=("arbitrary", "arbitrary"), vmem_limit_bytes=VMEM_ATTN),
        name="diff_attn",
    )(bound, lq1, lk1, lq2, lk2, gsub_t, q_t, k, v_t)


def _seq_dft_kernel(cq_ref, sq_ref, jrev_ref, y1_ref, y2_ref, o_ref):
    half = SEQ // 2
    norm = 1.0 / math.sqrt(SEQ * FOURIER_GROUP)
    jrev = jrev_ref[...]
    n_tiles = half // REV_TILE
    row = lax.broadcasted_iota(jnp.int32, (half, 1), 0)

    def reversed_shifted(hi):
        parts = [_dot(jrev, hi[(n_tiles - 1 - t) * REV_TILE:(n_tiles - t) * REV_TILE])
                 for t in range(n_tiles)]
        return pltpu.roll(jnp.concatenate(parts, axis=0), 1, 0)

    not_first = row != 0
    r1 = jnp.where(not_first, reversed_shifted(y1_ref[0, half:, :]), 0.0)
    r2 = jnp.where(not_first, reversed_shifted(y2_ref[0, half:, :]), 0.0)
    p = y1_ref[0, :half, :].astype(F32) + r1
    q = y2_ref[0, :half, :].astype(F32) - r2
    a = _dot(cq_ref[...], p.astype(BF16))
    b = _dot(sq_ref[...], q.astype(BF16))
    alt = jnp.where(row % 2 == 0, 1.0, -1.0)
    y_mid = y1_ref[0, half:half + 1, :].astype(F32) * norm
    corr = alt * y_mid
    o_ref[0, :half, :] = (a - b + corr).astype(BF16)
    upper = reversed_shifted((a + b + corr).astype(BF16))
    mid = norm * jnp.sum(alt * p, axis=0, keepdims=True) + y_mid
    o_ref[0, half:, :] = jnp.where(not_first, upper, mid).astype(BF16)


def _seq_dft(cq, sq, jrev, y1, y2):
    n_seq = y1.shape[0]
    seq = pl.BlockSpec((1, SEQ, FOURIER_WIDTH), lambda b: (b, 0, 0))
    return pl.pallas_call(
        _seq_dft_kernel,
        grid=(n_seq,),
        in_specs=[_resident((SEQ // 2, SEQ // 2)), _resident((SEQ // 2, SEQ // 2)),
                  _resident((REV_TILE, REV_TILE)), seq, seq],
        out_specs=seq,
        out_shape=jax.ShapeDtypeStruct((n_seq, SEQ, FOURIER_WIDTH), BF16),
        compiler_params=pltpu.CompilerParams(
            dimension_semantics=("arbitrary",), vmem_limit_bytes=VMEM_DFT),
        name="seq_dft",
    )(cq, sq, jrev, y1, y2)


def _merge_mlp_kernel(x_ref, att_ref, four_ref, gmix_ref, wgate_ref, bgate_ref, wattn_ref,
                      wfour_ref, wout_ref, gmlp_ref, wup_ref, wdown_ref, y_ref):
    for r in range(OUT_TILE // OUT_HALF):
        rows = slice(r * OUT_HALF, (r + 1) * OUT_HALF)
        x = x_ref[rows, :]
        h = _rmsnorm_rows(x, gmix_ref[...]).astype(BF16)
        gates = jax.nn.sigmoid(_dot(h, wgate_ref[...]) + bgate_ref[...])
        attn_out = _dot(att_ref[rows, :], wattn_ref[...])
        four_out = _dot(four_ref[rows, :], wfour_ref[...])
        mixed = gates[:, :D_MODEL] * attn_out + gates[:, D_MODEL:] * four_out
        x1 = x + _dot(mixed.astype(BF16), wout_ref[...])
        h2 = _rmsnorm_rows(x1, gmlp_ref[...]).astype(BF16)
        u = jnp.square(jnp.maximum(_dot(h2, wup_ref[...]), 0.0)).astype(BF16)
        y_ref[rows, :] = x1 + _dot(u, wdown_ref[...])


def _merge_mlp(x, att, four, gmix, wgate, bgate, wattn, wfour, wout, gmlp, wup, wdown):
    n_tok = x.shape[0]
    tm = OUT_TILE
    tok = lambda w: pl.BlockSpec((tm, w), lambda i: (i, 0))
    return pl.pallas_call(
        _merge_mlp_kernel,
        grid=(n_tok // tm,),
        in_specs=[tok(D_MODEL), tok(V_WIDTH), tok(FOURIER_WIDTH), _resident((1, D_MODEL)),
                  _resident((D_MODEL, 2 * D_MODEL)), _resident((1, 2 * D_MODEL)),
                  _resident((V_WIDTH, D_MODEL)), _resident((FOURIER_WIDTH, D_MODEL)),
                  _resident((D_MODEL, D_MODEL)), _resident((1, D_MODEL)),
                  _resident((D_MODEL, D_FF)), _resident((D_FF, D_MODEL))],
        out_specs=tok(D_MODEL),
        out_shape=jax.ShapeDtypeStruct((n_tok, D_MODEL), F32),
        compiler_params=pltpu.CompilerParams(
            dimension_semantics=("arbitrary",), vmem_limit_bytes=VMEM_MERGE),
        name="merge_mlp",
    )(x, att, four, gmix, wgate, bgate, wattn, wfour, wout, gmlp, wup, wdown)


def _rope_tables():
    half = HEAD_DIM // 2
    freqs = ROPE_THETA ** (-jnp.arange(half, dtype=F32) * 2.0 / HEAD_DIM)
    ang = jnp.arange(SEQ, dtype=F32)[:, None] * freqs[None, :]
    return jnp.cos(ang).T, jnp.sin(ang).T


def _dft_angles(n):
    idx = jnp.arange(n, dtype=jnp.int32)
    kn = (idx[:, None] * idx[None, :]) % n
    return kn.astype(F32) * (2.0 * math.pi / n)


def _seq_dft_tables(norm):
    half = SEQ // 2
    k = jnp.arange(half, dtype=jnp.int32)[:, None]
    step = 2.0 * math.pi / SEQ
    ang_a = ((k * (jnp.arange(half // LANES, dtype=jnp.int32) * LANES)[None, :]) % SEQ).astype(F32) * step
    ang_b = ((k * jnp.arange(LANES, dtype=jnp.int32)[None, :]) % SEQ).astype(F32) * step
    ca, sa = jnp.cos(ang_a)[:, :, None] * norm, jnp.sin(ang_a)[:, :, None] * norm
    cb, sb = jnp.cos(ang_b)[:, None, :], jnp.sin(ang_b)[:, None, :]
    cq = (ca * cb - sa * sb).reshape(half, half)
    sq = (sa * cb + ca * sb).reshape(half, half)
    return cq.astype(BF16), sq.astype(BF16)


def _dft_tables():
    norm = 1.0 / math.sqrt(SEQ * FOURIER_GROUP)
    cq, sq = _seq_dft_tables(norm)
    jrev = jnp.flip(jnp.eye(REV_TILE, dtype=BF16), axis=0)
    ang_c = _dft_angles(FOURIER_GROUP)
    dc = jnp.concatenate([jnp.cos(ang_c), jnp.sin(ang_c)], axis=1).astype(BF16)
    return cq, sq, jrev, dc


def kernel(x_prompt, x_sample, g_mix, w_in, g_q, g_k, lam_q1, lam_k1, lam_q2, lam_k2, g_sub,
           w_attn_br, w_four_br, w_gate, b_gate, w_out, g_mlp, w_up, w_down):
    assert x_prompt.shape[1:] == (SEQ, D_MODEL) and x_sample.shape[1:] == (SEQ, D_MODEL)
    assert w_in.shape == (1, D_MODEL, IN_WIDTH) and w_gate.shape == (1, D_MODEL, 2 * D_MODEL)
    assert w_up.shape == (1, D_MODEL, D_FF) and w_down.shape == (1, D_FF, D_MODEL)
    assert g_q.shape == (1, HEAD_DIM) and g_sub.shape == (1, V_HEAD_DIM)
    row = lambda p: p[0].reshape(1, -1).astype(F32)
    as_bf16 = lambda w: w[0].astype(BF16)
    cos_t, sin_t = _rope_tables()
    cq, sq, jrev, dc = _dft_tables()
    col = lambda p: jnp.broadcast_to(p[0].astype(F32)[:, None], (p.shape[1], IN_HALF))
    gq = col(g_q) * (HEAD_DIM ** -0.5 * math.log2(math.e))
    gk = col(g_k)
    score_bound = (HEAD_DIM * jnp.max(jnp.abs(gq)) * jnp.max(jnp.abs(gk))).reshape(1)
    gsub_t = jnp.broadcast_to(g_sub[0].astype(F32)[:, None] * (1.0 - LAMBDA_INIT),
                              (V_HEAD_DIM, Q_TILE))
    gmix = row(g_mix)
    lams = (row(lam_q1), row(lam_k1), row(lam_q2), row(lam_k2))
    w_in_b, w_gate_b, w_attn_b, w_four_b, w_out_b, w_up_b, w_down_b = map(
        as_bf16, (w_in, w_gate, w_attn_br, w_four_br, w_out, w_up, w_down))

    def layer(x3):
        n_seq = x3.shape[0]
        x = x3.reshape(n_seq * SEQ, D_MODEL)
        q_t, k, v_t, y1, y2 = _inproj(x, gmix, w_in_b, gq, gk, cos_t, sin_t, dc)
        seq3 = lambda a: a.reshape(n_seq, SEQ, a.shape[-1])
        att = _attention(score_bound, *lams, gsub_t, q_t, seq3(k), v_t)
        four = _seq_dft(cq, sq, jrev, seq3(y1), seq3(y2))
        y = _merge_mlp(x, att.reshape(n_seq * SEQ, V_WIDTH),
                       four.reshape(n_seq * SEQ, FOURIER_WIDTH), gmix, w_gate_b, row(b_gate),
                       w_attn_b, w_four_b, w_out_b, row(g_mlp), w_up_b, w_down_b)
        return y.reshape(n_seq, SEQ, D_MODEL)

    return layer(x_prompt), layer(x_sample)
```

```python
import functools
import math

import jax
import jax.numpy as jnp
from jax import lax
from jax.experimental import pallas as pl
from jax.experimental.pallas import tpu as pltpu

D_MODEL = 1024
SEQ = 2048
N_HEADS = 8
HEAD_DIM = 64
V_HEAD_DIM = 2 * HEAD_DIM
QK_WIDTH = 2 * N_HEADS * HEAD_DIM
V_WIDTH = N_HEADS * V_HEAD_DIM
N_FOURIER_GROUPS = 4
FOURIER_GROUP = 128
FOURIER_WIDTH = N_FOURIER_GROUPS * FOURIER_GROUP
IN_WIDTH = 2 * QK_WIDTH + V_WIDTH + FOURIER_WIDTH
D_FF = 4 * D_MODEL
ROPE_THETA = 10000.0
EPS = 1e-6
LAMBDA_INIT = 0.8 - 0.6 * math.exp(-0.3 * 0)

LANES = 128
IN_TILE = 1024
IN_HALF = 512
IN_COLS = 256
OUT_TILE = 1024
OUT_HALF = 512
Q_TILE = 512
ATTN_HEADS = 2
REV_TILE = 256
SCORE_BOUND_LIMIT = 48.0
MIB = 1024 * 1024
VMEM_INPROJ = 48 * MIB
VMEM_ATTN = 40 * MIB
VMEM_DFT = 48 * MIB
VMEM_MERGE = 56 * MIB

BF16 = jnp.bfloat16
F32 = jnp.float32


def _dot(a, b):
    return jnp.dot(a, b, preferred_element_type=F32)


def _rmsnorm_rows(x, g):
    ms = jnp.mean(x * x, axis=-1, keepdims=True)
    return x * lax.rsqrt(ms + EPS) * g


def _resident(shape):
    return pl.BlockSpec(shape, lambda *_: (0,) * len(shape), pipeline_mode=pl.Buffered(1))


def _map_norm_rope_t(z, gain, cos, sin):
    ss = jnp.sum(z * z, axis=0, keepdims=True)
    zn = z * lax.rsqrt(ss * (1.0 / HEAD_DIM) + EPS) * gain
    half = HEAD_DIM // 2
    x1, x2 = zn[:half], zn[half:]
    return x1 * cos - x2 * sin, x2 * cos + x1 * sin


def _inproj_kernel(x_ref, gmix_ref, w_ref, gq_ref, gk_ref, cos_ref, sin_ref, dc_ref,
                   qt_ref, k_ref, vt_ref, y1_ref, y2_ref):
    half = HEAD_DIM // 2
    dc = dc_ref[...]
    gq = gq_ref[...]
    gk = gk_ref[...]
    v_off = 2 * QK_WIDTH

    def normed_maps(z, gain, cos, sin):
        z_t = z.T
        return [_map_norm_rope_t(z_t[m * HEAD_DIM:(m + 1) * HEAD_DIM], gain, cos, sin)
                for m in range(IN_COLS // HEAD_DIM)]

    for r in range(IN_TILE // IN_HALF):
        rows = slice(r * IN_HALF, (r + 1) * IN_HALF)
        h = _rmsnorm_rows(x_ref[rows, :], gmix_ref[...]).astype(BF16)
        cos = cos_ref[:, rows]
        sin = sin_ref[:, rows]

        z = _dot(h, w_ref[:, :QK_WIDTH])
        for c in range(QK_WIDTH // IN_COLS):
            for m, (lo, hi) in enumerate(normed_maps(z[:, c * IN_COLS:(c + 1) * IN_COLS],
                                                     gq, cos, sin)):
                r0 = c * IN_COLS + m * HEAD_DIM
                qt_ref[0, r0:r0 + half, rows] = lo.astype(BF16)
                qt_ref[0, r0 + half:r0 + HEAD_DIM, rows] = hi.astype(BF16)

        z = _dot(h, w_ref[:, QK_WIDTH:2 * QK_WIDTH])
        for c in range(QK_WIDTH // IN_COLS):
            cols = slice(c * IN_COLS, (c + 1) * IN_COLS)
            pieces = [p for lo_hi in normed_maps(z[:, cols], gk, cos, sin) for p in lo_hi]
            k_ref[rows, cols] = jnp.concatenate(pieces, axis=0).T.astype(BF16)

        vt_ref[0, :, rows] = _dot(h, w_ref[:, v_off:v_off + V_WIDTH]).astype(BF16).T

        f = _dot(h, w_ref[:, v_off + V_WIDTH:IN_WIDTH]).astype(BF16)
        for g in range(N_FOURIER_GROUPS):
            sl = slice(g * FOURIER_GROUP, (g + 1) * FOURIER_GROUP)
            y = _dot(f[:, sl], dc)
            y1_ref[rows, sl] = y[:, :FOURIER_GROUP].astype(BF16)
            y2_ref[rows, sl] = y[:, FOURIER_GROUP:].astype(BF16)


def _inproj(x, gmix, w_in, gq, gk, cos_t, sin_t, dc):
    n_tok = x.shape[0]
    n_seq = n_tok // SEQ
    tm = IN_TILE
    pos_blocks = SEQ // tm
    tok = lambda w: pl.BlockSpec((tm, w), lambda i: (i, 0))
    rope = pl.BlockSpec((HEAD_DIM // 2, tm), lambda i: (0, i % pos_blocks))
    dims_by_tok = lambda w: pl.BlockSpec((1, w, tm), lambda i: (i // pos_blocks, 0, i % pos_blocks))
    out_shapes = (
        jax.ShapeDtypeStruct((n_seq, QK_WIDTH, SEQ), BF16),
        jax.ShapeDtypeStruct((n_tok, QK_WIDTH), BF16),
        jax.ShapeDtypeStruct((n_seq, V_WIDTH, SEQ), BF16),
        jax.ShapeDtypeStruct((n_tok, FOURIER_WIDTH), BF16),
        jax.ShapeDtypeStruct((n_tok, FOURIER_WIDTH), BF16),
    )
    return pl.pallas_call(
        _inproj_kernel,
        grid=(n_tok // tm,),
        in_specs=[tok(D_MODEL), _resident((1, D_MODEL)), _resident((D_MODEL, IN_WIDTH)),
                  _resident((HEAD_DIM, IN_HALF)), _resident((HEAD_DIM, IN_HALF)), rope, rope,
                  _resident((FOURIER_GROUP, 2 * FOURIER_GROUP))],
        out_specs=(dims_by_tok(QK_WIDTH), tok(QK_WIDTH), dims_by_tok(V_WIDTH),
                   tok(FOURIER_WIDTH), tok(FOURIER_WIDTH)),
        out_shape=out_shapes,
        compiler_params=pltpu.CompilerParams(
            dimension_semantics=("arbitrary",), vmem_limit_bytes=VMEM_INPROJ,
            allow_input_fusion=[False, False, True, False, False, False, False, False]),
        name="inproj",
    )(x, gmix, w_in, gq, gk, cos_t, sin_t, dc)


def _attn_kernel(bound_ref, lq1_ref, lk1_ref, lq2_ref, lk2_ref, gsub_ref, qt_ref, k_ref, vt_ref,
                 o_ref, lam_ref):
    @pl.when((pl.program_id(0) == 0) & (pl.program_id(1) == 0))
    def _():
        a1 = jnp.sum(lq1_ref[...] * lk1_ref[...], axis=-1, keepdims=True)
        a2 = jnp.sum(lq2_ref[...] * lk2_ref[...], axis=-1, keepdims=True)
        lam_ref[...] = jnp.exp(a1) - jnp.exp(a2) + LAMBDA_INIT

    lam = lam_ref[...]
    gsub = gsub_ref[...]
    tq = Q_TILE
    first_map = lax.broadcasted_iota(jnp.int32, (2 * HEAD_DIM, tq), 0) < HEAD_DIM

    def block(hh, i, shift_by_max):
        head = slice(hh * V_HEAD_DIM, (hh + 1) * V_HEAD_DIM)
        k = k_ref[0, :, head]
        v_t = vt_ref[0, head, :]
        cols = pl.ds(pl.multiple_of(i * tq, tq), tq)
        q_t = qt_ref[0, head, cols]
        zero = jnp.zeros_like(q_t)
        qq_t = jnp.concatenate([jnp.where(first_map, q_t, zero), jnp.where(first_map, zero, q_t)],
                               axis=1)
        s_t = _dot(k, qq_t)
        if shift_by_max:
            s_t = s_t - jnp.max(s_t, axis=0, keepdims=True)
        e = jnp.exp2(s_t)
        inv = 1.0 / jnp.sum(e, axis=0, keepdims=True)
        acc = _dot(v_t, e.astype(BF16))
        o_t = acc[:, :tq] * inv[:, :tq] - acc[:, tq:] * (lam * inv[:, tq:])
        ms = jnp.mean(o_t * o_t, axis=0, keepdims=True)
        o_n = o_t * lax.rsqrt(ms + EPS) * gsub
        o_ref[0, cols, head] = o_n.T.astype(BF16)

    def all_blocks(shift_by_max):
        for hh in range(ATTN_HEADS):
            def body(i, carry):
                block(hh, i, shift_by_max)
                return carry
            lax.fori_loop(0, SEQ // tq, body, 0, unroll=SEQ // tq if not shift_by_max else 1)

    no_shift_needed = bound_ref[0] <= SCORE_BOUND_LIMIT
    pl.when(no_shift_needed)(functools.partial(all_blocks, False))
    pl.when(jnp.logical_not(no_shift_needed))(functools.partial(all_blocks, True))


def _attention(bound, lq1, lk1, lq2, lk2, gsub_t, q_t, k, v_t):
    n_seq = q_t.shape[0]
    width = ATTN_HEADS * V_HEAD_DIM
    head = pl.BlockSpec((1, SEQ, width), lambda b, h: (b, 0, h))
    head_t = pl.BlockSpec((1, width, SEQ), lambda b, h: (b, h, 0))
    return pl.pallas_call(
        _attn_kernel,
        grid=(n_seq, N_HEADS // ATTN_HEADS),
        in_specs=[pl.BlockSpec(memory_space=pltpu.SMEM)] + [_resident((1, HEAD_DIM))] * 4
        + [_resident((V_HEAD_DIM, Q_TILE)), head_t, head, head_t],
        out_specs=head,
        out_shape=jax.ShapeDtypeStruct((n_seq, SEQ, V_WIDTH), BF16),
        scratch_shapes=[pltpu.VMEM((1, 1), F32)],
        compiler_params=pltpu.CompilerParams(
            dimension_semantics=("arbitrary", "arbitrary"), vmem_limit_bytes=VMEM_ATTN),
        name="diff_attn",
    )(bound, lq1, lk1, lq2, lk2, gsub_t, q_t, k, v_t)


def _seq_dft_kernel(cq_ref, sq_ref, jrev_ref, y1_ref, y2_ref, o_ref):
    half = SEQ // 2
    norm = 1.0 / math.sqrt(SEQ * FOURIER_GROUP)
    jrev = jrev_ref[...]
    n_tiles = half // REV_TILE
    row = lax.broadcasted_iota(jnp.int32, (half, 1), 0)

    def reversed_shifted(hi):
        parts = [_dot(jrev, hi[(n_tiles - 1 - t) * REV_TILE:(n_tiles - t) * REV_TILE])
                 for t in range(n_tiles)]
        return pltpu.roll(jnp.concatenate(parts, axis=0), 1, 0)

    not_first = row != 0
    r1 = jnp.where(not_first, reversed_shifted(y1_ref[0, half:, :]), 0.0)
    r2 = jnp.where(not_first, reversed_shifted(y2_ref[0, half:, :]), 0.0)
    p = y1_ref[0, :half, :].astype(F32) + r1
    q = y2_ref[0, :half, :].astype(F32) - r2
    a = _dot(cq_ref[...], p.astype(BF16))
    b = _dot(sq_ref[...], q.astype(BF16))
    alt = jnp.where(row % 2 == 0, 1.0, -1.0)
    y_mid = y1_ref[0, half:half + 1, :].astype(F32) * norm
    corr = alt * y_mid
    o_ref[0, :half, :] = (a - b + corr).astype(BF16)
    upper = reversed_shifted((a + b + corr).astype(BF16))
    mid = norm * jnp.sum(alt * p, axis=0, keepdims=True) + y_mid
    o_ref[0, half:, :] = jnp.where(not_first, upper, mid).astype(BF16)


def _seq_dft(cq, sq, jrev, y1, y2):
    n_seq = y1.shape[0]
    seq = pl.BlockSpec((1, SEQ, FOURIER_WIDTH), lambda b: (b, 0, 0))
    return pl.pallas_call(
        _seq_dft_kernel,
        grid=(n_seq,),
        in_specs=[_resident((SEQ // 2, SEQ // 2)), _resident((SEQ // 2, SEQ // 2)),
                  _resident((REV_TILE, REV_TILE)), seq, seq],
        out_specs=seq,
        out_shape=jax.ShapeDtypeStruct((n_seq, SEQ, FOURIER_WIDTH), BF16),
        compiler_params=pltpu.CompilerParams(
            dimension_semantics=("arbitrary",), vmem_limit_bytes=VMEM_DFT),
        name="seq_dft",
    )(cq, sq, jrev, y1, y2)


def _merge_mlp_kernel(x_ref, att_ref, four_ref, gmix_ref, wgate_ref, bgate_ref, wattn_ref,
                      wfour_ref, wout_ref, gmlp_ref, wup_ref, wdown_ref, y_ref):
    for r in range(OUT_TILE // OUT_HALF):
        rows = slice(r * OUT_HALF, (r + 1) * OUT_HALF)
        x = x_ref[rows, :]
        h = _rmsnorm_rows(x, gmix_ref[...]).astype(BF16)
        gates = jax.nn.sigmoid(_dot(h, wgate_ref[...]) + bgate_ref[...])
        attn_out = _dot(att_ref[rows, :], wattn_ref[...])
        four_out = _dot(four_ref[rows, :], wfour_ref[...])
        mixed = gates[:, :D_MODEL] * attn_out + gates[:, D_MODEL:] * four_out
        x1 = x + _dot(mixed.astype(BF16), wout_ref[...])
        h2 = _rmsnorm_rows(x1, gmlp_ref[...]).astype(BF16)
        u = jnp.square(jnp.maximum(_dot(h2, wup_ref[...]), 0.0)).astype(BF16)
        y_ref[rows, :] = x1 + _dot(u, wdown_ref[...])


def _merge_mlp(x, att, four, gmix, wgate, bgate, wattn, wfour, wout, gmlp, wup, wdown):
    n_tok = x.shape[0]
    tm = OUT_TILE
    tok = lambda w: pl.BlockSpec((tm, w), lambda i: (i, 0))
    return pl.pallas_call(
        _merge_mlp_kernel,
        grid=(n_tok // tm,),
        in_specs=[tok(D_MODEL), tok(V_WIDTH), tok(FOURIER_WIDTH), _resident((1, D_MODEL)),
                  _resident((D_MODEL, 2 * D_MODEL)), _resident((1, 2 * D_MODEL)),
                  _resident((V_WIDTH, D_MODEL)), _resident((FOURIER_WIDTH, D_MODEL)),
                  _resident((D_MODEL, D_MODEL)), _resident((1, D_MODEL)),
                  _resident((D_MODEL, D_FF)), _resident((D_FF, D_MODEL))],
        out_specs=tok(D_MODEL),
        out_shape=jax.ShapeDtypeStruct((n_tok, D_MODEL), F32),
        compiler_params=pltpu.CompilerParams(
            dimension_semantics=("arbitrary",), vmem_limit_bytes=VMEM_MERGE,
            allow_input_fusion=[False, False, False, False, True, False, True, True, True, False,
                                True, True]),
        name="merge_mlp",
    )(x, att, four, gmix, wgate, bgate, wattn, wfour, wout, gmlp, wup, wdown)


def _rope_tables():
    half = HEAD_DIM // 2
    freqs = ROPE_THETA ** (-jnp.arange(half, dtype=F32) * 2.0 / HEAD_DIM)
    ang = jnp.arange(SEQ, dtype=F32)[:, None] * freqs[None, :]
    return jnp.cos(ang).T, jnp.sin(ang).T


def _dft_angles(n):
    idx = jnp.arange(n, dtype=jnp.int32)
    kn = (idx[:, None] * idx[None, :]) % n
    return kn.astype(F32) * (2.0 * math.pi / n)


def _seq_dft_tables(norm):
    half = SEQ // 2
    k = jnp.arange(half, dtype=jnp.int32)[:, None]
    step = 2.0 * math.pi / SEQ
    ang_a = ((k * (jnp.arange(half // LANES, dtype=jnp.int32) * LANES)[None, :]) % SEQ).astype(F32) * step
    ang_b = ((k * jnp.arange(LANES, dtype=jnp.int32)[None, :]) % SEQ).astype(F32) * step
    ca, sa = jnp.cos(ang_a)[:, :, None] * norm, jnp.sin(ang_a)[:, :, None] * norm
    cb, sb = jnp.cos(ang_b)[:, None, :], jnp.sin(ang_b)[:, None, :]
    cq = (ca * cb - sa * sb).reshape(half, half)
    sq = (sa * cb + ca * sb).reshape(half, half)
    return cq.astype(BF16), sq.astype(BF16)


def _dft_tables():
    norm = 1.0 / math.sqrt(SEQ * FOURIER_GROUP)
    cq, sq = _seq_dft_tables(norm)
    jrev = jnp.flip(jnp.eye(REV_TILE, dtype=BF16), axis=0)
    ang_c = _dft_angles(FOURIER_GROUP)
    dc = jnp.concatenate([jnp.cos(ang_c), jnp.sin(ang_c)], axis=1).astype(BF16)
    return cq, sq, jrev, dc


def kernel(x_prompt, x_sample, g_mix, w_in, g_q, g_k, lam_q1, lam_k1, lam_q2, lam_k2, g_sub,
           w_attn_br, w_four_br, w_gate, b_gate, w_out, g_mlp, w_up, w_down):
    assert x_prompt.shape[1:] == (SEQ, D_MODEL) and x_sample.shape[1:] == (SEQ, D_MODEL)
    assert w_in.shape == (1, D_MODEL, IN_WIDTH) and w_gate.shape == (1, D_MODEL, 2 * D_MODEL)
    assert w_up.shape == (1, D_MODEL, D_FF) and w_down.shape == (1, D_FF, D_MODEL)
    assert g_q.shape == (1, HEAD_DIM) and g_sub.shape == (1, V_HEAD_DIM)
    row = lambda p: p[0].reshape(1, -1).astype(F32)
    as_bf16 = lambda w: w[0].astype(BF16)
    cos_t, sin_t = _rope_tables()
    cq, sq, jrev, dc = _dft_tables()
    col = lambda p: jnp.broadcast_to(p[0].astype(F32)[:, None], (p.shape[1], IN_HALF))
    gq = col(g_q) * (HEAD_DIM ** -0.5 * math.log2(math.e))
    gk = col(g_k)
    score_bound = (HEAD_DIM * jnp.max(jnp.abs(gq)) * jnp.max(jnp.abs(gk))).reshape(1)
    gsub_t = jnp.broadcast_to(g_sub[0].astype(F32)[:, None] * (1.0 - LAMBDA_INIT),
                              (V_HEAD_DIM, Q_TILE))
    gmix = row(g_mix)
    lams = (row(lam_q1), row(lam_k1), row(lam_q2), row(lam_k2))
    w_in_b, w_gate_b, w_attn_b, w_four_b, w_out_b, w_up_b, w_down_b = map(
        as_bf16, (w_in, w_gate, w_attn_br, w_four_br, w_out, w_up, w_down))

    def layer(x3):
        n_seq = x3.shape[0]
        x = x3.reshape(n_seq * SEQ, D_MODEL)
        q_t, k, v_t, y1, y2 = _inproj(x, gmix, w_in_b, gq, gk, cos_t, sin_t, dc)
        seq3 = lambda a: a.reshape(n_seq, SEQ, a.shape[-1])
        att = _attention(score_bound, *lams, gsub_t, q_t, seq3(k), v_t)
        four = _seq_dft(cq, sq, jrev, seq3(y1), seq3(y2))
        y = _merge_mlp(x, att.reshape(n_seq * SEQ, V_WIDTH),
                       four.reshape(n_seq * SEQ, FOURIER_WIDTH), gmix, w_gate_b, row(b_gate),
                       w_attn_b, w_four_b, w_out_b, row(g_mlp), w_up_b, w_down_b)
        return y.reshape(n_seq, SEQ, D_MODEL)

    return layer(x_prompt), layer(x_sample)
```

```python
import functools
import math

import jax
import jax.numpy as jnp
from jax import lax
from jax.experimental import pallas as pl
from jax.experimental.pallas import tpu as pltpu

D_MODEL = 1024
SEQ = 2048
N_HEADS = 8
HEAD_DIM = 64
V_HEAD_DIM = 2 * HEAD_DIM
QK_WIDTH = 2 * N_HEADS * HEAD_DIM
V_WIDTH = N_HEADS * V_HEAD_DIM
N_FOURIER_GROUPS = 4
FOURIER_GROUP = 128
FOURIER_WIDTH = N_FOURIER_GROUPS * FOURIER_GROUP
IN_WIDTH = 2 * QK_WIDTH + V_WIDTH + FOURIER_WIDTH
D_FF = 4 * D_MODEL
ROPE_THETA = 10000.0
EPS = 1e-6
LAMBDA_INIT = 0.8 - 0.6 * math.exp(-0.3 * 0)

LANES = 128
IN_TILE = 1024
IN_HALF = 512
IN_COLS = 256
OUT_TILE = 1024
OUT_HALF = 512
Q_TILE = 512
ATTN_HEADS = 2
REV_TILE = 256
SCORE_BOUND_LIMIT = 48.0
MIB = 1024 * 1024
VMEM_INPROJ = 48 * MIB
VMEM_ATTN = 40 * MIB
VMEM_DFT = 48 * MIB
VMEM_MERGE = 56 * MIB

BF16 = jnp.bfloat16
F32 = jnp.float32


def _dot(a, b):
    return jnp.dot(a, b, preferred_element_type=F32)


def _rmsnorm_rows(x, g):
    ms = jnp.mean(x * x, axis=-1, keepdims=True)
    return x * lax.rsqrt(ms + EPS) * g


def _resident(shape):
    return pl.BlockSpec(shape, lambda *_: (0,) * len(shape), pipeline_mode=pl.Buffered(1))


def _map_norm_rope_t(z, gain, cos, sin):
    ss = jnp.sum(z * z, axis=0, keepdims=True)
    zn = z * lax.rsqrt(ss * (1.0 / HEAD_DIM) + EPS) * gain
    half = HEAD_DIM // 2
    x1, x2 = zn[:half], zn[half:]
    return x1 * cos - x2 * sin, x2 * cos + x1 * sin


def _inproj_kernel(tiles_a, xa_ref, xb_ref, gmix_ref, w_ref, gq_ref, gk_ref, cos_ref, sin_ref,
                   dc_ref, qt_ref, k_ref, vt_ref, y1_ref, y2_ref):
    from_a = pl.program_id(0) < tiles_a
    half = HEAD_DIM // 2
    dc = dc_ref[...]
    gq = gq_ref[...]
    gk = gk_ref[...]
    v_off = 2 * QK_WIDTH

    def normed_maps(z, gain, cos, sin):
        z_t = z.T
        return [_map_norm_rope_t(z_t[m * HEAD_DIM:(m + 1) * HEAD_DIM], gain, cos, sin)
                for m in range(IN_COLS // HEAD_DIM)]

    for r in range(IN_TILE // IN_HALF):
        rows = slice(r * IN_HALF, (r + 1) * IN_HALF)
        x = jnp.where(from_a, xa_ref[rows, :], xb_ref[rows, :])
        h = _rmsnorm_rows(x, gmix_ref[...]).astype(BF16)
        cos = cos_ref[:, rows]
        sin = sin_ref[:, rows]

        z = _dot(h, w_ref[:, :QK_WIDTH])
        for c in range(QK_WIDTH // IN_COLS):
            for m, (lo, hi) in enumerate(normed_maps(z[:, c * IN_COLS:(c + 1) * IN_COLS],
                                                     gq, cos, sin)):
                r0 = c * IN_COLS + m * HEAD_DIM
                qt_ref[0, r0:r0 + half, rows] = lo.astype(BF16)
                qt_ref[0, r0 + half:r0 + HEAD_DIM, rows] = hi.astype(BF16)

        z = _dot(h, w_ref[:, QK_WIDTH:2 * QK_WIDTH])
        for c in range(QK_WIDTH // IN_COLS):
            cols = slice(c * IN_COLS, (c + 1) * IN_COLS)
            pieces = [p for lo_hi in normed_maps(z[:, cols], gk, cos, sin) for p in lo_hi]
            k_ref[rows, cols] = jnp.concatenate(pieces, axis=0).T.astype(BF16)

        vt_ref[0, :, rows] = _dot(h, w_ref[:, v_off:v_off + V_WIDTH]).astype(BF16).T

        f = _dot(h, w_ref[:, v_off + V_WIDTH:IN_WIDTH]).astype(BF16)
        for g in range(N_FOURIER_GROUPS):
            sl = slice(g * FOURIER_GROUP, (g + 1) * FOURIER_GROUP)
            y = _dot(f[:, sl], dc)
            y1_ref[rows, sl] = y[:, :FOURIER_GROUP].astype(BF16)
            y2_ref[rows, sl] = y[:, FOURIER_GROUP:].astype(BF16)


def _inproj(x_a, x_b, gmix, w_in, gq, gk, cos_t, sin_t, dc):
    n_tok = x_a.shape[0] + x_b.shape[0]
    n_seq = n_tok // SEQ
    tm = IN_TILE
    tiles_a = x_a.shape[0] // tm
    pos_blocks = SEQ // tm
    tok = lambda w: pl.BlockSpec((tm, w), lambda i: (i, 0))
    tok_a = pl.BlockSpec((tm, D_MODEL), lambda i: (jnp.minimum(i, tiles_a - 1), 0))
    tok_b = pl.BlockSpec((tm, D_MODEL), lambda i: (jnp.maximum(i - tiles_a, 0), 0))
    rope = pl.BlockSpec((HEAD_DIM // 2, tm), lambda i: (0, i % pos_blocks))
    dims_by_tok = lambda w: pl.BlockSpec((1, w, tm), lambda i: (i // pos_blocks, 0, i % pos_blocks))
    out_shapes = (
        jax.ShapeDtypeStruct((n_seq, QK_WIDTH, SEQ), BF16),
        jax.ShapeDtypeStruct((n_tok, QK_WIDTH), BF16),
        jax.ShapeDtypeStruct((n_seq, V_WIDTH, SEQ), BF16),
        jax.ShapeDtypeStruct((n_tok, FOURIER_WIDTH), BF16),
        jax.ShapeDtypeStruct((n_tok, FOURIER_WIDTH), BF16),
    )
    return pl.pallas_call(
        functools.partial(_inproj_kernel, tiles_a),
        grid=(n_tok // tm,),
        in_specs=[tok_a, tok_b, _resident((1, D_MODEL)), _resident((D_MODEL, IN_WIDTH)),
                  _resident((HEAD_DIM, IN_HALF)), _resident((HEAD_DIM, IN_HALF)), rope, rope,
                  _resident((FOURIER_GROUP, 2 * FOURIER_GROUP))],
        out_specs=(dims_by_tok(QK_WIDTH), tok(QK_WIDTH), dims_by_tok(V_WIDTH),
                   tok(FOURIER_WIDTH), tok(FOURIER_WIDTH)),
        out_shape=out_shapes,
        compiler_params=pltpu.CompilerParams(
            dimension_semantics=("arbitrary",), vmem_limit_bytes=VMEM_INPROJ),
        name="inproj",
    )(x_a, x_b, gmix, w_in, gq, gk, cos_t, sin_t, dc)


def _attn_kernel(bound_ref, lq1_ref, lk1_ref, lq2_ref, lk2_ref, gsub_ref, qt_ref, k_ref, vt_ref,
                 o_ref, lam_ref):
    @pl.when((pl.program_id(0) == 0) & (pl.program_id(1) == 0))
    def _():
        a1 = jnp.sum(lq1_ref[...] * lk1_ref[...], axis=-1, keepdims=True)
        a2 = jnp.sum(lq2_ref[...] * lk2_ref[...], axis=-1, keepdims=True)
        lam_ref[...] = jnp.exp(a1) - jnp.exp(a2) + LAMBDA_INIT

    lam = lam_ref[...]
    gsub = gsub_ref[...]
    tq = Q_TILE
    first_map = lax.broadcasted_iota(jnp.int32, (2 * HEAD_DIM, tq), 0) < HEAD_DIM

    def block(hh, i, shift_by_max):
        head = slice(hh * V_HEAD_DIM, (hh + 1) * V_HEAD_DIM)
        k = k_ref[0, :, head]
        v_t = vt_ref[0, head, :]
        cols = pl.ds(pl.multiple_of(i * tq, tq), tq)
        q_t = qt_ref[0, head, cols]
        zero = jnp.zeros_like(q_t)
        qq_t = jnp.concatenate([jnp.where(first_map, q_t, zero), jnp.where(first_map, zero, q_t)],
                               axis=1)
        s_t = _dot(k, qq_t)
        if shift_by_max:
            s_t = s_t - jnp.max(s_t, axis=0, keepdims=True)
        e = jnp.exp2(s_t)
        inv = 1.0 / jnp.sum(e, axis=0, keepdims=True)
        acc = _dot(v_t, e.astype(BF16))
        o_t = acc[:, :tq] * inv[:, :tq] - acc[:, tq:] * (lam * inv[:, tq:])
        ms = jnp.mean(o_t * o_t, axis=0, keepdims=True)
        o_n = o_t * lax.rsqrt(ms + EPS) * gsub
        o_ref[0, cols, head] = o_n.T.astype(BF16)

    def all_blocks(shift_by_max):
        for hh in range(ATTN_HEADS):
            def body(i, carry):
                block(hh, i, shift_by_max)
                return carry
            lax.fori_loop(0, SEQ // tq, body, 0, unroll=SEQ // tq if not shift_by_max else 1)

    no_shift_needed = bound_ref[0] <= SCORE_BOUND_LIMIT
    pl.when(no_shift_needed)(functools.partial(all_blocks, False))
    pl.when(jnp.logical_not(no_shift_needed))(functools.partial(all_blocks, True))


def _attention(bound, lq1, lk1, lq2, lk2, gsub_t, q_t, k, v_t):
    n_seq = q_t.shape[0]
    width = ATTN_HEADS * V_HEAD_DIM
    head = pl.BlockSpec((1, SEQ, width), lambda b, h: (b, 0, h))
    head_t = pl.BlockSpec((1, width, SEQ), lambda b, h: (b, h, 0))
    return pl.pallas_call(
        _attn_kernel,
        grid=(n_seq, N_HEADS // ATTN_HEADS),
        in_specs=[pl.BlockSpec(memory_space=pltpu.SMEM)] + [_resident((1, HEAD_DIM))] * 4
        + [_resident((V_HEAD_DIM, Q_TILE)), head_t, head, head_t],
        out_specs=head,
        out_shape=jax.ShapeDtypeStruct((n_seq, SEQ, V_WIDTH), BF16),
        scratch_shapes=[pltpu.VMEM((1, 1), F32)],
        compiler_params=pltpu.CompilerParams(
            dimension_semantics=("arbitrary", "arbitrary"), vmem_limit_bytes=VMEM_ATTN),
        name="diff_attn",
    )(bound, lq1, lk1, lq2, lk2, gsub_t, q_t, k, v_t)


def _seq_dft_kernel(cq_ref, sq_ref, jrev_ref, y1_ref, y2_ref, o_ref):
    half = SEQ // 2
    norm = 1.0 / math.sqrt(SEQ * FOURIER_GROUP)
    jrev = jrev_ref[...]
    n_tiles = half // REV_TILE
    row = lax.broadcasted_iota(jnp.int32, (half, 1), 0)

    def reversed_shifted(hi):
        parts = [_dot(jrev, hi[(n_tiles - 1 - t) * REV_TILE:(n_tiles - t) * REV_TILE])
                 for t in range(n_tiles)]
        return pltpu.roll(jnp.concatenate(parts, axis=0), 1, 0)

    not_first = row != 0
    r1 = jnp.where(not_first, reversed_shifted(y1_ref[0, half:, :]), 0.0)
    r2 = jnp.where(not_first, reversed_shifted(y2_ref[0, half:, :]), 0.0)
    p = y1_ref[0, :half, :].astype(F32) + r1
    q = y2_ref[0, :half, :].astype(F32) - r2
    a = _dot(cq_ref[...], p.astype(BF16))
    b = _dot(sq_ref[...], q.astype(BF16))
    alt = jnp.where(row % 2 == 0, 1.0, -1.0)
    y_mid = y1_ref[0, half:half + 1, :].astype(F32) * norm
    corr = alt * y_mid
    o_ref[0, :half, :] = (a - b + corr).astype(BF16)
    upper = reversed_shifted((a + b + corr).astype(BF16))
    mid = norm * jnp.sum(alt * p, axis=0, keepdims=True) + y_mid
    o_ref[0, half:, :] = jnp.where(not_first, upper, mid).astype(BF16)


def _seq_dft(cq, sq, jrev, y1, y2):
    n_seq = y1.shape[0]
    seq = pl.BlockSpec((1, SEQ, FOURIER_WIDTH), lambda b: (b, 0, 0))
    return pl.pallas_call(
        _seq_dft_kernel,
        grid=(n_seq,),
        in_specs=[_resident((SEQ // 2, SEQ // 2)), _resident((SEQ // 2, SEQ // 2)),
                  _resident((REV_TILE, REV_TILE)), seq, seq],
        out_specs=seq,
        out_shape=jax.ShapeDtypeStruct((n_seq, SEQ, FOURIER_WIDTH), BF16),
        compiler_params=pltpu.CompilerParams(
            dimension_semantics=("arbitrary",), vmem_limit_bytes=VMEM_DFT),
        name="seq_dft",
    )(cq, sq, jrev, y1, y2)


def _merge_mlp_kernel(x_ref, att_ref, four_ref, gmix_ref, wgate_ref, bgate_ref, wattn_ref,
                      wfour_ref, wout_ref, gmlp_ref, wup_ref, wdown_ref, y_ref):
    for r in range(OUT_TILE // OUT_HALF):
        rows = slice(r * OUT_HALF, (r + 1) * OUT_HALF)
        x = x_ref[rows, :]
        h = _rmsnorm_rows(x, gmix_ref[...]).astype(BF16)
        gates = jax.nn.sigmoid(_dot(h, wgate_ref[...]) + bgate_ref[...])
        attn_out = _dot(att_ref[rows, :], wattn_ref[...])
        four_out = _dot(four_ref[rows, :], wfour_ref[...])
        mixed = gates[:, :D_MODEL] * attn_out + gates[:, D_MODEL:] * four_out
        x1 = x + _dot(mixed.astype(BF16), wout_ref[...])
        h2 = _rmsnorm_rows(x1, gmlp_ref[...]).astype(BF16)
        u = jnp.square(jnp.maximum(_dot(h2, wup_ref[...]), 0.0)).astype(BF16)
        y_ref[rows, :] = x1 + _dot(u, wdown_ref[...])


def _merge_mlp(x, tok_offset, att, four, gmix, wgate, bgate, wattn, wfour, wout, gmlp, wup,
               wdown):
    n_tok = x.shape[0]
    tm = OUT_TILE
    first = tok_offset // tm
    tok = lambda w: pl.BlockSpec((tm, w), lambda i: (i, 0))
    shared = lambda w: pl.BlockSpec((tm, w), lambda i: (i + first, 0))
    return pl.pallas_call(
        _merge_mlp_kernel,
        grid=(n_tok // tm,),
        in_specs=[tok(D_MODEL), shared(V_WIDTH), shared(FOURIER_WIDTH), _resident((1, D_MODEL)),
                  _resident((D_MODEL, 2 * D_MODEL)), _resident((1, 2 * D_MODEL)),
                  _resident((V_WIDTH, D_MODEL)), _resident((FOURIER_WIDTH, D_MODEL)),
                  _resident((D_MODEL, D_MODEL)), _resident((1, D_MODEL)),
                  _resident((D_MODEL, D_FF)), _resident((D_FF, D_MODEL))],
        out_specs=tok(D_MODEL),
        out_shape=jax.ShapeDtypeStruct((n_tok, D_MODEL), F32),
        compiler_params=pltpu.CompilerParams(
            dimension_semantics=("arbitrary",), vmem_limit_bytes=VMEM_MERGE),
        name="merge_mlp",
    )(x, att, four, gmix, wgate, bgate, wattn, wfour, wout, gmlp, wup, wdown)


def _rope_tables():
    half = HEAD_DIM // 2
    freqs = ROPE_THETA ** (-jnp.arange(half, dtype=F32) * 2.0 / HEAD_DIM)
    ang = jnp.arange(SEQ, dtype=F32)[:, None] * freqs[None, :]
    return jnp.cos(ang).T, jnp.sin(ang).T


def _dft_angles(n):
    idx = jnp.arange(n, dtype=jnp.int32)
    kn = (idx[:, None] * idx[None, :]) % n
    return kn.astype(F32) * (2.0 * math.pi / n)


def _seq_dft_tables(norm):
    half = SEQ // 2
    k = jnp.arange(half, dtype=jnp.int32)[:, None]
    step = 2.0 * math.pi / SEQ
    ang_a = ((k * (jnp.arange(half // LANES, dtype=jnp.int32) * LANES)[None, :]) % SEQ).astype(F32) * step
    ang_b = ((k * jnp.arange(LANES, dtype=jnp.int32)[None, :]) % SEQ).astype(F32) * step
    ca, sa = jnp.cos(ang_a)[:, :, None] * norm, jnp.sin(ang_a)[:, :, None] * norm
    cb, sb = jnp.cos(ang_b)[:, None, :], jnp.sin(ang_b)[:, None, :]
    cq = (ca * cb - sa * sb).reshape(half, half)
    sq = (sa * cb + ca * sb).reshape(half, half)
    return cq.astype(BF16), sq.astype(BF16)


def _dft_tables():
    norm = 1.0 / math.sqrt(SEQ * FOURIER_GROUP)
    cq, sq = _seq_dft_tables(norm)
    jrev = jnp.flip(jnp.eye(REV_TILE, dtype=BF16), axis=0)
    ang_c = _dft_angles(FOURIER_GROUP)
    dc = jnp.concatenate([jnp.cos(ang_c), jnp.sin(ang_c)], axis=1).astype(BF16)
    return cq, sq, jrev, dc


def kernel(x_prompt, x_sample, g_mix, w_in, g_q, g_k, lam_q1, lam_k1, lam_q2, lam_k2, g_sub,
           w_attn_br, w_four_br, w_gate, b_gate, w_out, g_mlp, w_up, w_down):
    assert x_prompt.shape[1:] == (SEQ, D_MODEL) and x_sample.shape[1:] == (SEQ, D_MODEL)
    assert w_in.shape == (1, D_MODEL, IN_WIDTH) and w_gate.shape == (1, D_MODEL, 2 * D_MODEL)
    assert w_up.shape == (1, D_MODEL, D_FF) and w_down.shape == (1, D_FF, D_MODEL)
    assert g_q.shape == (1, HEAD_DIM) and g_sub.shape == (1, V_HEAD_DIM)
    row = lambda p: p[0].reshape(1, -1).astype(F32)
    as_bf16 = lambda w: w[0].astype(BF16)
    cos_t, sin_t = _rope_tables()
    cq, sq, jrev, dc = _dft_tables()
    col = lambda p: jnp.broadcast_to(p[0].astype(F32)[:, None], (p.shape[1], IN_HALF))
    gq = col(g_q) * (HEAD_DIM ** -0.5 * math.log2(math.e))
    gk = col(g_k)
    score_bound = (HEAD_DIM * jnp.max(jnp.abs(gq)) * jnp.max(jnp.abs(gk))).reshape(1)
    gsub_t = jnp.broadcast_to(g_sub[0].astype(F32)[:, None] * (1.0 - LAMBDA_INIT),
                              (V_HEAD_DIM, Q_TILE))
    gmix = row(g_mix)
    lams = (row(lam_q1), row(lam_k1), row(lam_q2), row(lam_k2))
    w_in_b, w_gate_b, w_attn_b, w_four_b, w_out_b, w_up_b, w_down_b = map(
        as_bf16, (w_in, w_gate, w_attn_br, w_four_br, w_out, w_up, w_down))

    flat = lambda x3: x3.reshape(x3.shape[0] * SEQ, D_MODEL)
    x_a, x_b = flat(x_prompt), flat(x_sample)
    n_seq = x_prompt.shape[0] + x_sample.shape[0]
    q_t, k, v_t, y1, y2 = _inproj(x_a, x_b, gmix, w_in_b, gq, gk, cos_t, sin_t, dc)
    seq3 = lambda a: a.reshape(n_seq, SEQ, a.shape[-1])
    att = _attention(score_bound, *lams, gsub_t, q_t, seq3(k), v_t).reshape(n_seq * SEQ, V_WIDTH)
    four = _seq_dft(cq, sq, jrev, seq3(y1), seq3(y2)).reshape(n_seq * SEQ, FOURIER_WIDTH)

    def merge(x, tok_offset):
        y = _merge_mlp(x, tok_offset, att, four, gmix, w_gate_b, row(b_gate), w_attn_b, w_four_b,
                       w_out_b, row(g_mlp), w_up_b, w_down_b)
        return y.reshape(-1, SEQ, D_MODEL)

    return merge(x_a, 0), merge(x_b, x_a.shape[0])
```

```python
import functools
import math

import jax
import jax.numpy as jnp
from jax import lax
from jax.experimental import pallas as pl
from jax.experimental.pallas import tpu as pltpu

D_MODEL = 1024
SEQ = 2048
N_HEADS = 8
HEAD_DIM = 64
V_HEAD_DIM = 2 * HEAD_DIM
QK_WIDTH = 2 * N_HEADS * HEAD_DIM
V_WIDTH = N_HEADS * V_HEAD_DIM
N_FOURIER_GROUPS = 4
FOURIER_GROUP = 128
FOURIER_WIDTH = N_FOURIER_GROUPS * FOURIER_GROUP
IN_WIDTH = 2 * QK_WIDTH + V_WIDTH + FOURIER_WIDTH
D_FF = 4 * D_MODEL
ROPE_THETA = 10000.0
EPS = 1e-6
LAMBDA_INIT = 0.8 - 0.6 * math.exp(-0.3 * 0)

LANES = 128
IN_TILE = 1024
IN_HALF = 512
IN_COLS = 256
OUT_TILE = 1024
OUT_HALF = 512
FF_CHUNK = 2048
Q_TILE = 512
ATTN_HEADS = 2
REV_TILE = 256
SCORE_BOUND_LIMIT = 48.0
MIB = 1024 * 1024
VMEM_INPROJ = 48 * MIB
VMEM_ATTN = 40 * MIB
VMEM_DFT = 48 * MIB
VMEM_MERGE = 56 * MIB

BF16 = jnp.bfloat16
F32 = jnp.float32


def _dot(a, b):
    return jnp.dot(a, b, preferred_element_type=F32)


def _rmsnorm_rows(x, g):
    ms = jnp.mean(x * x, axis=-1, keepdims=True)
    return x * lax.rsqrt(ms + EPS) * g


def _resident(shape):
    return pl.BlockSpec(shape, lambda *_: (0,) * len(shape), pipeline_mode=pl.Buffered(1))


def _map_norm_rope_t(z, gain, cos, sin):
    ss = jnp.sum(z * z, axis=0, keepdims=True)
    zn = z * lax.rsqrt(ss * (1.0 / HEAD_DIM) + EPS) * gain
    half = HEAD_DIM // 2
    x1, x2 = zn[:half], zn[half:]
    return x1 * cos - x2 * sin, x2 * cos + x1 * sin


def _inproj_kernel(x_ref, gmix_ref, w_ref, gq_ref, gk_ref, cos_ref, sin_ref, dc_ref,
                   qt_ref, k_ref, vt_ref, y1_ref, y2_ref):
    half = HEAD_DIM // 2
    dc = dc_ref[...]
    gq = gq_ref[...]
    gk = gk_ref[...]
    v_off = 2 * QK_WIDTH

    def normed_maps(z, gain, cos, sin):
        z_t = z.T
        return [_map_norm_rope_t(z_t[m * HEAD_DIM:(m + 1) * HEAD_DIM], gain, cos, sin)
                for m in range(IN_COLS // HEAD_DIM)]

    for r in range(IN_TILE // IN_HALF):
        rows = slice(r * IN_HALF, (r + 1) * IN_HALF)
        h = _rmsnorm_rows(x_ref[rows, :], gmix_ref[...]).astype(BF16)
        cos = cos_ref[:, rows]
        sin = sin_ref[:, rows]

        z = _dot(h, w_ref[:, :QK_WIDTH])
        for c in range(QK_WIDTH // IN_COLS):
            for m, (lo, hi) in enumerate(normed_maps(z[:, c * IN_COLS:(c + 1) * IN_COLS],
                                                     gq, cos, sin)):
                r0 = c * IN_COLS + m * HEAD_DIM
                qt_ref[0, r0:r0 + half, rows] = lo.astype(BF16)
                qt_ref[0, r0 + half:r0 + HEAD_DIM, rows] = hi.astype(BF16)

        z = _dot(h, w_ref[:, QK_WIDTH:2 * QK_WIDTH])
        for c in range(QK_WIDTH // IN_COLS):
            cols = slice(c * IN_COLS, (c + 1) * IN_COLS)
            pieces = [p for lo_hi in normed_maps(z[:, cols], gk, cos, sin) for p in lo_hi]
            k_ref[rows, cols] = jnp.concatenate(pieces, axis=0).T.astype(BF16)

        vt_ref[0, :, rows] = _dot(h, w_ref[:, v_off:v_off + V_WIDTH]).astype(BF16).T

        f = _dot(h, w_ref[:, v_off + V_WIDTH:IN_WIDTH]).astype(BF16)
        for g in range(N_FOURIER_GROUPS):
            sl = slice(g * FOURIER_GROUP, (g + 1) * FOURIER_GROUP)
            y = _dot(f[:, sl], dc)
            y1_ref[rows, sl] = y[:, :FOURIER_GROUP].astype(BF16)
            y2_ref[rows, sl] = y[:, FOURIER_GROUP:].astype(BF16)


def _inproj(x, gmix, w_in, gq, gk, cos_t, sin_t, dc):
    n_tok = x.shape[0]
    n_seq = n_tok // SEQ
    tm = IN_TILE
    pos_blocks = SEQ // tm
    tok = lambda w: pl.BlockSpec((tm, w), lambda i: (i, 0))
    rope = pl.BlockSpec((HEAD_DIM // 2, tm), lambda i: (0, i % pos_blocks))
    dims_by_tok = lambda w: pl.BlockSpec((1, w, tm), lambda i: (i // pos_blocks, 0, i % pos_blocks))
    out_shapes = (
        jax.ShapeDtypeStruct((n_seq, QK_WIDTH, SEQ), BF16),
        jax.ShapeDtypeStruct((n_tok, QK_WIDTH), BF16),
        jax.ShapeDtypeStruct((n_seq, V_WIDTH, SEQ), BF16),
        jax.ShapeDtypeStruct((n_tok, FOURIER_WIDTH), BF16),
        jax.ShapeDtypeStruct((n_tok, FOURIER_WIDTH), BF16),
    )
    return pl.pallas_call(
        _inproj_kernel,
        grid=(n_tok // tm,),
        in_specs=[tok(D_MODEL), _resident((1, D_MODEL)), _resident((D_MODEL, IN_WIDTH)),
                  _resident((HEAD_DIM, IN_HALF)), _resident((HEAD_DIM, IN_HALF)), rope, rope,
                  _resident((FOURIER_GROUP, 2 * FOURIER_GROUP))],
        out_specs=(dims_by_tok(QK_WIDTH), tok(QK_WIDTH), dims_by_tok(V_WIDTH),
                   tok(FOURIER_WIDTH), tok(FOURIER_WIDTH)),
        out_shape=out_shapes,
        compiler_params=pltpu.CompilerParams(
            dimension_semantics=("arbitrary",), vmem_limit_bytes=VMEM_INPROJ),
        name="inproj",
    )(x, gmix, w_in, gq, gk, cos_t, sin_t, dc)


def _attn_kernel(bound_ref, lq1_ref, lk1_ref, lq2_ref, lk2_ref, gsub_ref, qt_ref, k_ref, vt_ref,
                 o_ref, lam_ref):
    @pl.when((pl.program_id(0) == 0) & (pl.program_id(1) == 0))
    def _():
        a1 = jnp.sum(lq1_ref[...] * lk1_ref[...], axis=-1, keepdims=True)
        a2 = jnp.sum(lq2_ref[...] * lk2_ref[...], axis=-1, keepdims=True)
        lam_ref[...] = jnp.exp(a1) - jnp.exp(a2) + LAMBDA_INIT

    lam = lam_ref[...]
    gsub = gsub_ref[...]
    tq = Q_TILE
    first_map = lax.broadcasted_iota(jnp.int32, (2 * HEAD_DIM, tq), 0) < HEAD_DIM

    def block(hh, i, shift_by_max):
        head = slice(hh * V_HEAD_DIM, (hh + 1) * V_HEAD_DIM)
        k = k_ref[0, :, head]
        v_t = vt_ref[0, head, :]
        cols = pl.ds(pl.multiple_of(i * tq, tq), tq)
        q_t = qt_ref[0, head, cols]
        zero = jnp.zeros_like(q_t)
        qq_t = jnp.concatenate([jnp.where(first_map, q_t, zero), jnp.where(first_map, zero, q_t)],
                               axis=1)
        s_t = _dot(k, qq_t)
        if shift_by_max:
            s_t = s_t - jnp.max(s_t, axis=0, keepdims=True)
        e = jnp.exp2(s_t)
        inv = 1.0 / jnp.sum(e, axis=0, keepdims=True)
        acc = _dot(v_t, e.astype(BF16))
        o_t = acc[:, :tq] * inv[:, :tq] - acc[:, tq:] * (lam * inv[:, tq:])
        ms = jnp.mean(o_t * o_t, axis=0, keepdims=True)
        o_n = o_t * lax.rsqrt(ms + EPS) * gsub
        o_ref[0, cols, head] = o_n.T.astype(BF16)

    def all_blocks(shift_by_max):
        for hh in range(ATTN_HEADS):
            def body(i, carry):
                block(hh, i, shift_by_max)
                return carry
            lax.fori_loop(0, SEQ // tq, body, 0, unroll=SEQ // tq if not shift_by_max else 1)

    no_shift_needed = bound_ref[0] <= SCORE_BOUND_LIMIT
    pl.when(no_shift_needed)(functools.partial(all_blocks, False))
    pl.when(jnp.logical_not(no_shift_needed))(functools.partial(all_blocks, True))


def _attention(bound, lq1, lk1, lq2, lk2, gsub_t, q_t, k, v_t):
    n_seq = q_t.shape[0]
    width = ATTN_HEADS * V_HEAD_DIM
    head = pl.BlockSpec((1, SEQ, width), lambda b, h: (b, 0, h))
    head_t = pl.BlockSpec((1, width, SEQ), lambda b, h: (b, h, 0))
    return pl.pallas_call(
        _attn_kernel,
        grid=(n_seq, N_HEADS // ATTN_HEADS),
        in_specs=[pl.BlockSpec(memory_space=pltpu.SMEM)] + [_resident((1, HEAD_DIM))] * 4
        + [_resident((V_HEAD_DIM, Q_TILE)), head_t, head, head_t],
        out_specs=head,
        out_shape=jax.ShapeDtypeStruct((n_seq, SEQ, V_WIDTH), BF16),
        scratch_shapes=[pltpu.VMEM((1, 1), F32)],
        compiler_params=pltpu.CompilerParams(
            dimension_semantics=("arbitrary", "arbitrary"), vmem_limit_bytes=VMEM_ATTN),
        name="diff_attn",
    )(bound, lq1, lk1, lq2, lk2, gsub_t, q_t, k, v_t)


def _seq_dft_kernel(cq_ref, sq_ref, jrev_ref, y1_ref, y2_ref, o_ref):
    half = SEQ // 2
    norm = 1.0 / math.sqrt(SEQ * FOURIER_GROUP)
    jrev = jrev_ref[...]
    n_tiles = half // REV_TILE
    row = lax.broadcasted_iota(jnp.int32, (half, 1), 0)

    def reversed_shifted(hi):
        parts = [_dot(jrev, hi[(n_tiles - 1 - t) * REV_TILE:(n_tiles - t) * REV_TILE])
                 for t in range(n_tiles)]
        return pltpu.roll(jnp.concatenate(parts, axis=0), 1, 0)

    not_first = row != 0
    r1 = jnp.where(not_first, reversed_shifted(y1_ref[0, half:, :]), 0.0)
    r2 = jnp.where(not_first, reversed_shifted(y2_ref[0, half:, :]), 0.0)
    p = y1_ref[0, :half, :].astype(F32) + r1
    q = y2_ref[0, :half, :].astype(F32) - r2
    a = _dot(cq_ref[...], p.astype(BF16))
    b = _dot(sq_ref[...], q.astype(BF16))
    alt = jnp.where(row % 2 == 0, 1.0, -1.0)
    y_mid = y1_ref[0, half:half + 1, :].astype(F32) * norm
    corr = alt * y_mid
    o_ref[0, :half, :] = (a - b + corr).astype(BF16)
    upper = reversed_shifted((a + b + corr).astype(BF16))
    mid = norm * jnp.sum(alt * p, axis=0, keepdims=True) + y_mid
    o_ref[0, half:, :] = jnp.where(not_first, upper, mid).astype(BF16)


def _seq_dft(cq, sq, jrev, y1, y2):
    n_seq = y1.shape[0]
    seq = pl.BlockSpec((1, SEQ, FOURIER_WIDTH), lambda b: (b, 0, 0))
    return pl.pallas_call(
        _seq_dft_kernel,
        grid=(n_seq,),
        in_specs=[_resident((SEQ // 2, SEQ // 2)), _resident((SEQ // 2, SEQ // 2)),
                  _resident((REV_TILE, REV_TILE)), seq, seq],
        out_specs=seq,
        out_shape=jax.ShapeDtypeStruct((n_seq, SEQ, FOURIER_WIDTH), BF16),
        compiler_params=pltpu.CompilerParams(
            dimension_semantics=("arbitrary",), vmem_limit_bytes=VMEM_DFT),
        name="seq_dft",
    )(cq, sq, jrev, y1, y2)


def _merge_mlp_kernel(x_ref, att_ref, four_ref, gmix_ref, wgate_ref, bgate_ref, wattn_ref,
                      wfour_ref, wout_ref, gmlp_ref, wup_ref, wdown_ref, y_ref):
    for r in range(OUT_TILE // OUT_HALF):
        rows = slice(r * OUT_HALF, (r + 1) * OUT_HALF)
        x = x_ref[rows, :]
        h = _rmsnorm_rows(x, gmix_ref[...]).astype(BF16)
        gates = jax.nn.sigmoid(_dot(h, wgate_ref[...]) + bgate_ref[...])
        attn_out = _dot(att_ref[rows, :], wattn_ref[...])
        four_out = _dot(four_ref[rows, :], wfour_ref[...])
        mixed = gates[:, :D_MODEL] * attn_out + gates[:, D_MODEL:] * four_out
        x1 = x + _dot(mixed.astype(BF16), wout_ref[...])
        h2 = _rmsnorm_rows(x1, gmlp_ref[...]).astype(BF16)
        y = x1
        for c in range(D_FF // FF_CHUNK):
            ff = slice(c * FF_CHUNK, (c + 1) * FF_CHUNK)
            u = jnp.square(jnp.maximum(_dot(h2, wup_ref[:, ff]), 0.0)).astype(BF16)
            y = y + _dot(u, wdown_ref[ff, :])
        y_ref[rows, :] = y


def _merge_mlp(x, att, four, gmix, wgate, bgate, wattn, wfour, wout, gmlp, wup, wdown):
    n_tok = x.shape[0]
    tm = OUT_TILE
    tok = lambda w: pl.BlockSpec((tm, w), lambda i: (i, 0))
    return pl.pallas_call(
        _merge_mlp_kernel,
        grid=(n_tok // tm,),
        in_specs=[tok(D_MODEL), tok(V_WIDTH), tok(FOURIER_WIDTH), _resident((1, D_MODEL)),
                  _resident((D_MODEL, 2 * D_MODEL)), _resident((1, 2 * D_MODEL)),
                  _resident((V_WIDTH, D_MODEL)), _resident((FOURIER_WIDTH, D_MODEL)),
                  _resident((D_MODEL, D_MODEL)), _resident((1, D_MODEL)),
                  _resident((D_MODEL, D_FF)), _resident((D_FF, D_MODEL))],
        out_specs=tok(D_MODEL),
        out_shape=jax.ShapeDtypeStruct((n_tok, D_MODEL), F32),
        compiler_params=pltpu.CompilerParams(
            dimension_semantics=("arbitrary",), vmem_limit_bytes=VMEM_MERGE),
        name="merge_mlp",
    )(x, att, four, gmix, wgate, bgate, wattn, wfour, wout, gmlp, wup, wdown)


def _rope_tables():
    half = HEAD_DIM // 2
    freqs = ROPE_THETA ** (-jnp.arange(half, dtype=F32) * 2.0 / HEAD_DIM)
    ang = jnp.arange(SEQ, dtype=F32)[:, None] * freqs[None, :]
    return jnp.cos(ang).T, jnp.sin(ang).T


def _dft_angles(n):
    idx = jnp.arange(n, dtype=jnp.int32)
    kn = (idx[:, None] * idx[None, :]) % n
    return kn.astype(F32) * (2.0 * math.pi / n)


def _seq_dft_tables(norm):
    half = SEQ // 2
    k = jnp.arange(half, dtype=jnp.int32)[:, None]
    step = 2.0 * math.pi / SEQ
    ang_a = ((k * (jnp.arange(half // LANES, dtype=jnp.int32) * LANES)[None, :]) % SEQ).astype(F32) * step
    ang_b = ((k * jnp.arange(LANES, dtype=jnp.int32)[None, :]) % SEQ).astype(F32) * step
    ca, sa = jnp.cos(ang_a)[:, :, None] * norm, jnp.sin(ang_a)[:, :, None] * norm
    cb, sb = jnp.cos(ang_b)[:, None, :], jnp.sin(ang_b)[:, None, :]
    cq = (ca * cb - sa * sb).reshape(half, half)
    sq = (sa * cb + ca * sb).reshape(half, half)
    return cq.astype(BF16), sq.astype(BF16)


def _dft_tables():
    norm = 1.0 / math.sqrt(SEQ * FOURIER_GROUP)
    cq, sq = _seq_dft_tables(norm)
    jrev = jnp.flip(jnp.eye(REV_TILE, dtype=BF16), axis=0)
    ang_c = _dft_angles(FOURIER_GROUP)
    dc = jnp.concatenate([jnp.cos(ang_c), jnp.sin(ang_c)], axis=1).astype(BF16)
    return cq, sq, jrev, dc


def kernel(x_prompt, x_sample, g_mix, w_in, g_q, g_k, lam_q1, lam_k1, lam_q2, lam_k2, g_sub,
           w_attn_br, w_four_br, w_gate, b_gate, w_out, g_mlp, w_up, w_down):
    assert x_prompt.shape[1:] == (SEQ, D_MODEL) and x_sample.shape[1:] == (SEQ, D_MODEL)
    assert w_in.shape == (1, D_MODEL, IN_WIDTH) and w_gate.shape == (1, D_MODEL, 2 * D_MODEL)
    assert w_up.shape == (1, D_MODEL, D_FF) and w_down.shape == (1, D_FF, D_MODEL)
    assert g_q.shape == (1, HEAD_DIM) and g_sub.shape == (1, V_HEAD_DIM)
    row = lambda p: p[0].reshape(1, -1).astype(F32)
    as_bf16 = lambda w: w[0].astype(BF16)
    cos_t, sin_t = _rope_tables()
    cq, sq, jrev, dc = _dft_tables()
    col = lambda p: jnp.broadcast_to(p[0].astype(F32)[:, None], (p.shape[1], IN_HALF))
    gq = col(g_q) * (HEAD_DIM ** -0.5 * math.log2(math.e))
    gk = col(g_k)
    score_bound = (HEAD_DIM * jnp.max(jnp.abs(gq)) * jnp.max(jnp.abs(gk))).reshape(1)
    gsub_t = jnp.broadcast_to(g_sub[0].astype(F32)[:, None] * (1.0 - LAMBDA_INIT),
                              (V_HEAD_DIM, Q_TILE))
    gmix = row(g_mix)
    lams = (row(lam_q1), row(lam_k1), row(lam_q2), row(lam_k2))
    w_in_b, w_gate_b, w_attn_b, w_four_b, w_out_b, w_up_b, w_down_b = map(
        as_bf16, (w_in, w_gate, w_attn_br, w_four_br, w_out, w_up, w_down))

    def layer(x3):
        n_seq = x3.shape[0]
        x = x3.reshape(n_seq * SEQ, D_MODEL)
        q_t, k, v_t, y1, y2 = _inproj(x, gmix, w_in_b, gq, gk, cos_t, sin_t, dc)
        seq3 = lambda a: a.reshape(n_seq, SEQ, a.shape[-1])
        att = _attention(score_bound, *lams, gsub_t, q_t, seq3(k), v_t)
        four = _seq_dft(cq, sq, jrev, seq3(y1), seq3(y2))
        y = _merge_mlp(x, att.reshape(n_seq * SEQ, V_WIDTH),
                       four.reshape(n_seq * SEQ, FOURIER_WIDTH), gmix, w_gate_b, row(b_gate),
                       w_attn_b, w_four_b, w_out_b, row(g_mlp), w_up_b, w_down_b)
        return y.reshape(n_seq, SEQ, D_MODEL)

    return layer(x_prompt), layer(x_sample)
```

```python
import functools
import math

import jax
import jax.numpy as jnp
from jax import lax
from jax.experimental import pallas as pl
from jax.experimental.pallas import tpu as pltpu

D_MODEL = 1024
SEQ = 2048
N_HEADS = 8
HEAD_DIM = 64
V_HEAD_DIM = 2 * HEAD_DIM
QK_WIDTH = 2 * N_HEADS * HEAD_DIM
V_WIDTH = N_HEADS * V_HEAD_DIM
N_FOURIER_GROUPS = 4
FOURIER_GROUP = 128
FOURIER_WIDTH = N_FOURIER_GROUPS * FOURIER_GROUP
IN_WIDTH = 2 * QK_WIDTH + V_WIDTH + FOURIER_WIDTH
D_FF = 4 * D_MODEL
ROPE_THETA = 10000.0
EPS = 1e-6
LAMBDA_INIT = 0.8 - 0.6 * math.exp(-0.3 * 0)

LANES = 128
IN_TILE = 1024
IN_HALF = 512
IN_COLS = 256
OUT_TILE = 1024
OUT_HALF = 512
Q_TILE = 512
ATTN_HEADS = 2
REV_TILE = 256
SCORE_BOUND_LIMIT = 48.0
MIB = 1024 * 1024
VMEM_INPROJ = 48 * MIB
VMEM_ATTN = 40 * MIB
VMEM_DFT = 48 * MIB
VMEM_MERGE = 56 * MIB

BF16 = jnp.bfloat16
F32 = jnp.float32


def _dot(a, b):
    return jnp.dot(a, b, preferred_element_type=F32)


def _rmsnorm_rows(x, g):
    ms = jnp.mean(x * x, axis=-1, keepdims=True)
    return x * lax.rsqrt(ms + EPS) * g


def _resident(shape):
    return pl.BlockSpec(shape, lambda *_: (0,) * len(shape), pipeline_mode=pl.Buffered(1))


def _map_norm_rope_t(z, gain, cos, sin):
    ss = jnp.sum(z * z, axis=0, keepdims=True)
    zn = z * lax.rsqrt(ss * (1.0 / HEAD_DIM) + EPS) * gain
    half = HEAD_DIM // 2
    x1, x2 = zn[:half], zn[half:]
    return x1 * cos - x2 * sin, x2 * cos + x1 * sin


def _inproj_kernel(x_ref, gmix_ref, w_ref, gq_ref, gk_ref, cos_ref, sin_ref, dc_ref,
                   qt_ref, k_ref, vt_ref, y1_ref, y2_ref):
    half = HEAD_DIM // 2
    dc = dc_ref[...]
    gq = gq_ref[...]
    gk = gk_ref[...]
    v_off = 2 * QK_WIDTH

    def normed_maps(z, gain, cos, sin):
        z_t = z.T
        return [_map_norm_rope_t(z_t[m * HEAD_DIM:(m + 1) * HEAD_DIM], gain, cos, sin)
                for m in range(IN_COLS // HEAD_DIM)]

    for r in range(IN_TILE // IN_HALF):
        rows = slice(r * IN_HALF, (r + 1) * IN_HALF)
        h = _rmsnorm_rows(x_ref[rows, :], gmix_ref[...]).astype(BF16)
        cos = cos_ref[:, rows]
        sin = sin_ref[:, rows]

        z = _dot(h, w_ref[:, :QK_WIDTH])
        for c in range(QK_WIDTH // IN_COLS):
            for m, (lo, hi) in enumerate(normed_maps(z[:, c * IN_COLS:(c + 1) * IN_COLS],
                                                     gq, cos, sin)):
                r0 = c * IN_COLS + m * HEAD_DIM
                qt_ref[0, r0:r0 + half, rows] = lo.astype(BF16)
                qt_ref[0, r0 + half:r0 + HEAD_DIM, rows] = hi.astype(BF16)

        z = _dot(h, w_ref[:, QK_WIDTH:2 * QK_WIDTH])
        for c in range(QK_WIDTH // IN_COLS):
            cols = slice(c * IN_COLS, (c + 1) * IN_COLS)
            pieces = [p for lo_hi in normed_maps(z[:, cols], gk, cos, sin) for p in lo_hi]
            k_ref[rows, cols] = jnp.concatenate(pieces, axis=0).T.astype(BF16)

        vt_ref[0, :, rows] = _dot(h, w_ref[:, v_off:v_off + V_WIDTH]).astype(BF16).T

        f = _dot(h, w_ref[:, v_off + V_WIDTH:IN_WIDTH]).astype(BF16)
        for g in range(N_FOURIER_GROUPS):
            sl = slice(g * FOURIER_GROUP, (g + 1) * FOURIER_GROUP)
            y = _dot(f[:, sl], dc)
            y1_ref[rows, sl] = y[:, :FOURIER_GROUP].astype(BF16)
            y2_ref[rows, sl] = y[:, FOURIER_GROUP:].astype(BF16)


def _inproj(x, gmix, w_in, gq, gk, cos_t, sin_t, dc):
    n_tok = x.shape[0]
    n_seq = n_tok // SEQ
    tm = IN_TILE
    pos_blocks = SEQ // tm
    tok = lambda w: pl.BlockSpec((tm, w), lambda i: (i, 0))
    rope = pl.BlockSpec((HEAD_DIM // 2, tm), lambda i: (0, i % pos_blocks))
    dims_by_tok = lambda w: pl.BlockSpec((1, w, tm), lambda i: (i // pos_blocks, 0, i % pos_blocks))
    out_shapes = (
        jax.ShapeDtypeStruct((n_seq, QK_WIDTH, SEQ), BF16),
        jax.ShapeDtypeStruct((n_tok, QK_WIDTH), BF16),
        jax.ShapeDtypeStruct((n_seq, V_WIDTH, SEQ), BF16),
        jax.ShapeDtypeStruct((n_tok, FOURIER_WIDTH), BF16),
        jax.ShapeDtypeStruct((n_tok, FOURIER_WIDTH), BF16),
    )
    return pl.pallas_call(
        _inproj_kernel,
        grid=(n_tok // tm,),
        in_specs=[tok(D_MODEL), _resident((1, D_MODEL)), _resident((D_MODEL, IN_WIDTH)),
                  _resident((HEAD_DIM, IN_HALF)), _resident((HEAD_DIM, IN_HALF)), rope, rope,
                  _resident((FOURIER_GROUP, 2 * FOURIER_GROUP))],
        out_specs=(dims_by_tok(QK_WIDTH), tok(QK_WIDTH), dims_by_tok(V_WIDTH),
                   tok(FOURIER_WIDTH), tok(FOURIER_WIDTH)),
        out_shape=out_shapes,
        compiler_params=pltpu.CompilerParams(
            dimension_semantics=("parallel",), vmem_limit_bytes=VMEM_INPROJ),
        name="inproj",
    )(x, gmix, w_in, gq, gk, cos_t, sin_t, dc)


def _attn_kernel(bound_ref, lq1_ref, lk1_ref, lq2_ref, lk2_ref, gsub_ref, qt_ref, k_ref, vt_ref,
                 o_ref, lam_ref):
    @pl.when((pl.program_id(0) == 0) & (pl.program_id(1) == 0))
    def _():
        a1 = jnp.sum(lq1_ref[...] * lk1_ref[...], axis=-1, keepdims=True)
        a2 = jnp.sum(lq2_ref[...] * lk2_ref[...], axis=-1, keepdims=True)
        lam_ref[...] = jnp.exp(a1) - jnp.exp(a2) + LAMBDA_INIT

    lam = lam_ref[...]
    gsub = gsub_ref[...]
    tq = Q_TILE
    first_map = lax.broadcasted_iota(jnp.int32, (2 * HEAD_DIM, tq), 0) < HEAD_DIM

    def block(hh, i, shift_by_max):
        head = slice(hh * V_HEAD_DIM, (hh + 1) * V_HEAD_DIM)
        k = k_ref[0, :, head]
        v_t = vt_ref[0, head, :]
        cols = pl.ds(pl.multiple_of(i * tq, tq), tq)
        q_t = qt_ref[0, head, cols]
        zero = jnp.zeros_like(q_t)
        qq_t = jnp.concatenate([jnp.where(first_map, q_t, zero), jnp.where(first_map, zero, q_t)],
                               axis=1)
        s_t = _dot(k, qq_t)
        if shift_by_max:
            s_t = s_t - jnp.max(s_t, axis=0, keepdims=True)
        e = jnp.exp2(s_t)
        inv = 1.0 / jnp.sum(e, axis=0, keepdims=True)
        acc = _dot(v_t, e.astype(BF16))
        o_t = acc[:, :tq] * inv[:, :tq] - acc[:, tq:] * (lam * inv[:, tq:])
        ms = jnp.mean(o_t * o_t, axis=0, keepdims=True)
        o_n = o_t * lax.rsqrt(ms + EPS) * gsub
        o_ref[0, cols, head] = o_n.T.astype(BF16)

    def all_blocks(shift_by_max):
        for hh in range(ATTN_HEADS):
            def body(i, carry):
                block(hh, i, shift_by_max)
                return carry
            lax.fori_loop(0, SEQ // tq, body, 0, unroll=SEQ // tq if not shift_by_max else 1)

    no_shift_needed = bound_ref[0] <= SCORE_BOUND_LIMIT
    pl.when(no_shift_needed)(functools.partial(all_blocks, False))
    pl.when(jnp.logical_not(no_shift_needed))(functools.partial(all_blocks, True))


def _attention(bound, lq1, lk1, lq2, lk2, gsub_t, q_t, k, v_t):
    n_seq = q_t.shape[0]
    width = ATTN_HEADS * V_HEAD_DIM
    head = pl.BlockSpec((1, SEQ, width), lambda b, h: (b, 0, h))
    head_t = pl.BlockSpec((1, width, SEQ), lambda b, h: (b, h, 0))
    return pl.pallas_call(
        _attn_kernel,
        grid=(n_seq, N_HEADS // ATTN_HEADS),
        in_specs=[pl.BlockSpec(memory_space=pltpu.SMEM)] + [_resident((1, HEAD_DIM))] * 4
        + [_resident((V_HEAD_DIM, Q_TILE)), head_t, head, head_t],
        out_specs=head,
        out_shape=jax.ShapeDtypeStruct((n_seq, SEQ, V_WIDTH), BF16),
        scratch_shapes=[pltpu.VMEM((1, 1), F32)],
        compiler_params=pltpu.CompilerParams(
            dimension_semantics=("arbitrary", "arbitrary"), vmem_limit_bytes=VMEM_ATTN),
        name="diff_attn",
    )(bound, lq1, lk1, lq2, lk2, gsub_t, q_t, k, v_t)


def _seq_dft_kernel(cq_ref, sq_ref, jrev_ref, y1_ref, y2_ref, o_ref):
    half = SEQ // 2
    norm = 1.0 / math.sqrt(SEQ * FOURIER_GROUP)
    jrev = jrev_ref[...]
    n_tiles = half // REV_TILE
    row = lax.broadcasted_iota(jnp.int32, (half, 1), 0)

    def reversed_shifted(hi):
        parts = [_dot(jrev, hi[(n_tiles - 1 - t) * REV_TILE:(n_tiles - t) * REV_TILE])
                 for t in range(n_tiles)]
        return pltpu.roll(jnp.concatenate(parts, axis=0), 1, 0)

    not_first = row != 0
    r1 = jnp.where(not_first, reversed_shifted(y1_ref[0, half:, :]), 0.0)
    r2 = jnp.where(not_first, reversed_shifted(y2_ref[0, half:, :]), 0.0)
    p = y1_ref[0, :half, :].astype(F32) + r1
    q = y2_ref[0, :half, :].astype(F32) - r2
    a = _dot(cq_ref[...], p.astype(BF16))
    b = _dot(sq_ref[...], q.astype(BF16))
    alt = jnp.where(row % 2 == 0, 1.0, -1.0)
    y_mid = y1_ref[0, half:half + 1, :].astype(F32) * norm
    corr = alt * y_mid
    o_ref[0, :half, :] = (a - b + corr).astype(BF16)
    upper = reversed_shifted((a + b + corr).astype(BF16))
    mid = norm * jnp.sum(alt * p, axis=0, keepdims=True) + y_mid
    o_ref[0, half:, :] = jnp.where(not_first, upper, mid).astype(BF16)


def _seq_dft(cq, sq, jrev, y1, y2):
    n_seq = y1.shape[0]
    seq = pl.BlockSpec((1, SEQ, FOURIER_WIDTH), lambda b: (b, 0, 0))
    return pl.pallas_call(
        _seq_dft_kernel,
        grid=(n_seq,),
        in_specs=[_resident((SEQ // 2, SEQ // 2)), _resident((SEQ // 2, SEQ // 2)),
                  _resident((REV_TILE, REV_TILE)), seq, seq],
        out_specs=seq,
        out_shape=jax.ShapeDtypeStruct((n_seq, SEQ, FOURIER_WIDTH), BF16),
        compiler_params=pltpu.CompilerParams(
            dimension_semantics=("parallel",), vmem_limit_bytes=VMEM_DFT),
        name="seq_dft",
    )(cq, sq, jrev, y1, y2)


def _merge_mlp_kernel(x_ref, att_ref, four_ref, gmix_ref, wgate_ref, bgate_ref, wattn_ref,
                      wfour_ref, wout_ref, gmlp_ref, wup_ref, wdown_ref, y_ref):
    for r in range(OUT_TILE // OUT_HALF):
        rows = slice(r * OUT_HALF, (r + 1) * OUT_HALF)
        x = x_ref[rows, :]
        h = _rmsnorm_rows(x, gmix_ref[...]).astype(BF16)
        gates = jax.nn.sigmoid(_dot(h, wgate_ref[...]) + bgate_ref[...])
        attn_out = _dot(att_ref[rows, :], wattn_ref[...])
        four_out = _dot(four_ref[rows, :], wfour_ref[...])
        mixed = gates[:, :D_MODEL] * attn_out + gates[:, D_MODEL:] * four_out
        x1 = x + _dot(mixed.astype(BF16), wout_ref[...])
        h2 = _rmsnorm_rows(x1, gmlp_ref[...]).astype(BF16)
        u = jnp.square(jnp.maximum(_dot(h2, wup_ref[...]), 0.0)).astype(BF16)
        y_ref[rows, :] = x1 + _dot(u, wdown_ref[...])


def _merge_mlp(x, att, four, gmix, wgate, bgate, wattn, wfour, wout, gmlp, wup, wdown):
    n_tok = x.shape[0]
    tm = OUT_TILE
    tok = lambda w: pl.BlockSpec((tm, w), lambda i: (i, 0))
    return pl.pallas_call(
        _merge_mlp_kernel,
        grid=(n_tok // tm,),
        in_specs=[tok(D_MODEL), tok(V_WIDTH), tok(FOURIER_WIDTH), _resident((1, D_MODEL)),
                  _resident((D_MODEL, 2 * D_MODEL)), _resident((1, 2 * D_MODEL)),
                  _resident((V_WIDTH, D_MODEL)), _resident((FOURIER_WIDTH, D_MODEL)),
                  _resident((D_MODEL, D_MODEL)), _resident((1, D_MODEL)),
                  _resident((D_MODEL, D_FF)), _resident((D_FF, D_MODEL))],
        out_specs=tok(D_MODEL),
        out_shape=jax.ShapeDtypeStruct((n_tok, D_MODEL), F32),
        compiler_params=pltpu.CompilerParams(
            dimension_semantics=("parallel",), vmem_limit_bytes=VMEM_MERGE),
        name="merge_mlp",
    )(x, att, four, gmix, wgate, bgate, wattn, wfour, wout, gmlp, wup, wdown)


def _rope_tables():
    half = HEAD_DIM // 2
    freqs = ROPE_THETA ** (-jnp.arange(half, dtype=F32) * 2.0 / HEAD_DIM)
    ang = jnp.arange(SEQ, dtype=F32)[:, None] * freqs[None, :]
    return jnp.cos(ang).T, jnp.sin(ang).T


def _dft_angles(n):
    idx = jnp.arange(n, dtype=jnp.int32)
    kn = (idx[:, None] * idx[None, :]) % n
    return kn.astype(F32) * (2.0 * math.pi / n)


def _seq_dft_tables(norm):
    half = SEQ // 2
    k = jnp.arange(half, dtype=jnp.int32)[:, None]
    step = 2.0 * math.pi / SEQ
    ang_a = ((k * (jnp.arange(half // LANES, dtype=jnp.int32) * LANES)[None, :]) % SEQ).astype(F32) * step
    ang_b = ((k * jnp.arange(LANES, dtype=jnp.int32)[None, :]) % SEQ).astype(F32) * step
    ca, sa = jnp.cos(ang_a)[:, :, None] * norm, jnp.sin(ang_a)[:, :, None] * norm
    cb, sb = jnp.cos(ang_b)[:, None, :], jnp.sin(ang_b)[:, None, :]
    cq = (ca * cb - sa * sb).reshape(half, half)
    sq = (sa * cb + ca * sb).reshape(half, half)
    return cq.astype(BF16), sq.astype(BF16)


def _dft_tables():
    norm = 1.0 / math.sqrt(SEQ * FOURIER_GROUP)
    cq, sq = _seq_dft_tables(norm)
    jrev = jnp.flip(jnp.eye(REV_TILE, dtype=BF16), axis=0)
    ang_c = _dft_angles(FOURIER_GROUP)
    dc = jnp.concatenate([jnp.cos(ang_c), jnp.sin(ang_c)], axis=1).astype(BF16)
    return cq, sq, jrev, dc


def kernel(x_prompt, x_sample, g_mix, w_in, g_q, g_k, lam_q1, lam_k1, lam_q2, lam_k2, g_sub,
           w_attn_br, w_four_br, w_gate, b_gate, w_out, g_mlp, w_up, w_down):
    assert x_prompt.shape[1:] == (SEQ, D_MODEL) and x_sample.shape[1:] == (SEQ, D_MODEL)
    assert w_in.shape == (1, D_MODEL, IN_WIDTH) and w_gate.shape == (1, D_MODEL, 2 * D_MODEL)
    assert w_up.shape == (1, D_MODEL, D_FF) and w_down.shape == (1, D_FF, D_MODEL)
    assert g_q.shape == (1, HEAD_DIM) and g_sub.shape == (1, V_HEAD_DIM)
    row = lambda p: p[0].reshape(1, -1).astype(F32)
    as_bf16 = lambda w: w[0].astype(BF16)
    cos_t, sin_t = _rope_tables()
    cq, sq, jrev, dc = _dft_tables()
    col = lambda p: jnp.broadcast_to(p[0].astype(F32)[:, None], (p.shape[1], IN_HALF))
    gq = col(g_q) * (HEAD_DIM ** -0.5 * math.log2(math.e))
    gk = col(g_k)
    score_bound = (HEAD_DIM * jnp.max(jnp.abs(gq)) * jnp.max(jnp.abs(gk))).reshape(1)
    gsub_t = jnp.broadcast_to(g_sub[0].astype(F32)[:, None] * (1.0 - LAMBDA_INIT),
                              (V_HEAD_DIM, Q_TILE))
    gmix = row(g_mix)
    lams = (row(lam_q1), row(lam_k1), row(lam_q2), row(lam_k2))
    w_in_b, w_gate_b, w_attn_b, w_four_b, w_out_b, w_up_b, w_down_b = map(
        as_bf16, (w_in, w_gate, w_attn_br, w_four_br, w_out, w_up, w_down))

    def layer(x3):
        n_seq = x3.shape[0]
        x = x3.reshape(n_seq * SEQ, D_MODEL)
        q_t, k, v_t, y1, y2 = _inproj(x, gmix, w_in_b, gq, gk, cos_t, sin_t, dc)
        seq3 = lambda a: a.reshape(n_seq, SEQ, a.shape[-1])
        att = _attention(score_bound, *lams, gsub_t, q_t, seq3(k), v_t)
        four = _seq_dft(cq, sq, jrev, seq3(y1), seq3(y2))
        y = _merge_mlp(x, att.reshape(n_seq * SEQ, V_WIDTH),
                       four.reshape(n_seq * SEQ, FOURIER_WIDTH), gmix, w_gate_b, row(b_gate),
                       w_attn_b, w_four_b, w_out_b, row(g_mlp), w_up_b, w_down_b)
        return y.reshape(n_seq, SEQ, D_MODEL)

    return layer(x_prompt), layer(x_sample)
```
